```python
import math
import jax, jax.numpy as jnp
from jax import lax
import numpy as np


D_MODEL = 2048
BATCH = 4
SEQ = 4096
DEPTH = 4

GRID_W = 64
CTX_LEN = 256
BRANCH_WIDTH = D_MODEL // 2
N_BRANCH = 3
A_DH = 64
A_DV = 2 * A_DH
A_HEADS = BRANCH_WIDTH // A_DV
A_QK_WIDTH = A_HEADS * 2 * A_DH
A_SCALE = A_DH ** -0.5
A_Q_BLOCK = 128
ROPE_THETA = 10000.0
ROPE_PAIRS = A_DH // 4
LAMBDA_STD = 0.1
B_CHUNK = 128
B_GROUPS = 8
B_GDIM = BRANCH_WIDTH // B_GROUPS
C_WINDOWS = (2, 4, 8, 16)
C_GROUPS = len(C_WINDOWS)
C_GDIM = BRANCH_WIDTH // C_GROUPS
FFN_HIDDEN = ((8 * D_MODEL + 3 * 256 - 1) // (3 * 256)) * 256
ALPHA = (2 * DEPTH) ** 0.25
BETA = (8 * DEPTH) ** -0.25
LN_EPS = 1e-6
Q_OFF = 0
K_OFF = Q_OFF + A_QK_WIDTH
V_OFF = K_OFF + A_QK_WIDTH
V_END = V_OFF + BRANCH_WIDTH
BU_OFF = V_END
C_OFF = BU_OFF + 2 * BRANCH_WIDTH
G_OFF = C_OFF + BRANCH_WIDTH
IN_WIDTH = G_OFF + N_BRANCH * D_MODEL

kernel_name = 'hybrid_diffattn_gmlp_pool_dit_block'

f32 = jnp.float32


def norm_only(x):
    xf = x.astype(f32)
    mu = jnp.mean(xf, -1, keepdims=True)
    var = jnp.mean(jnp.square(xf - mu), -1, keepdims=True)
    return ((xf - mu) * lax.rsqrt(var + LN_EPS)).astype(x.dtype)


def layer_norm(x, g, b):
    xf = x.astype(f32)
    mu = jnp.mean(xf, -1, keepdims=True)
    var = jnp.mean(jnp.square(xf - mu), -1, keepdims=True)
    return ((xf - mu) * lax.rsqrt(var + LN_EPS) * g + b).astype(x.dtype)


def rms_norm(x, g):
    xf = x.astype(f32)
    return (xf * lax.rsqrt(jnp.mean(jnp.square(xf), -1, keepdims=True) + LN_EPS) * g).astype(x.dtype)


def adaln(cond, w, b, n_chunks):
    width = n_chunks * D_MODEL
    return jax.nn.silu(cond) @ w[:, :width] + b[:width]


def axial_rope_tables(row, col):
    inv = ROPE_THETA ** (-jnp.arange(ROPE_PAIRS, dtype=f32) / ROPE_PAIRS)
    ang = jnp.stack([row.astype(f32)[:, None] * inv, col.astype(f32)[:, None] * inv], axis=1)
    return jnp.cos(ang), jnp.sin(ang)


def apply_rope(x, cos, sin):
    xr = x.reshape(x.shape[:-1] + (2, 2, ROPE_PAIRS)).astype(f32)
    x1, x2 = xr[..., 0, :], xr[..., 1, :]
    cb, sb = cos[:, None, None], sin[:, None, None]
    out = jnp.stack([x1 * cb - x2 * sb, x1 * sb + x2 * cb], axis=-2)
    return out.reshape(x.shape).astype(x.dtype)


def heads_qk(z):
    return z.reshape(z.shape[0], z.shape[1], A_HEADS, 2, A_DH)


def heads_v(z):
    return z.reshape(z.shape[0], z.shape[1], A_HEADS, A_DV)


def diff_attention(q, k, v, lam):
    s = jnp.einsum('bqhmd,bkhmd->bhmqk', q, k).astype(f32) * A_SCALE
    p = jax.nn.softmax(s, axis=-1)
    a = (p[:, :, 0] - lam * p[:, :, 1]).astype(v.dtype)
    return jnp.einsum('bhqk,bkhd->bqhd', a, v)


def latent_diff_attention(q, k, v, lam):
    bsz, n, h, m, d = q.shape
    nblk = n // A_Q_BLOCK
    qb = jnp.moveaxis(q.reshape(bsz, nblk, A_Q_BLOCK, h, m, d), 1, 0)
    ob = lax.map(lambda blk: diff_attention(blk, k, v, lam), qb)
    return jnp.moveaxis(ob, 0, 1).reshape(bsz, n, h, v.shape[-1])


def diff_post(o, g, lam_init):
    o = rms_norm(o, g) * (1.0 - lam_init)
    return o.reshape(o.shape[0], o.shape[1], BRANCH_WIDTH)


def gmlp_branch(z_uv, ln_g, ln_b, w_s, b_s):
    z = jax.nn.gelu(z_uv, approximate=False)
    u, v = z[..., :BRANCH_WIDTH], z[..., BRANCH_WIDTH:]
    v = layer_norm(v, ln_g, ln_b)
    bsz, n, _ = v.shape
    v = v.reshape(bsz, n // B_CHUNK, B_CHUNK, B_GROUPS, B_GDIM)
    s = jnp.einsum('gij,bnjgc->bnigc', w_s, v) + b_s.T[:, :, None]
    return u * s.reshape(bsz, n, BRANCH_WIDTH)


def pool_branch(z, w_pool, scale):
    bsz, n, _ = z.shape
    zf = z.reshape(bsz, n, C_GROUPS, C_GDIM).astype(f32)
    cs = jnp.concatenate([jnp.zeros_like(zf[:, :1]), jnp.cumsum(zf, axis=1)], axis=1)
    t = jnp.arange(n)
    pooled = []
    for g, w in enumerate(C_WINDOWS):
        lo = jnp.clip(t - w // 2, 0, n)
        hi = jnp.clip(t - w // 2 + w, 0, n)
        cs_g = cs[:, :, g]
        pooled.append((cs_g[:, hi] - cs_g[:, lo]) / (hi - lo).astype(f32)[None, :, None])
    d = (jnp.stack(pooled, axis=2) - zf).astype(z.dtype)
    y = jnp.einsum('blgc,gcd->blgd', d, w_pool)
    return y.reshape(bsz, n, BRANCH_WIDTH) * scale


def merge_branches(z_gate, y_a, y_b, y_c, w_branch, w_out):
    bsz, n, _ = z_gate.shape
    ys = jnp.stack([y_a, y_b, y_c], axis=2)
    proj = jnp.einsum('blnw,nwd->blnd', ys, w_branch)
    gates = jax.nn.sigmoid(z_gate.reshape(bsz, n, N_BRANCH, D_MODEL))
    return jnp.sum(gates * proj, axis=2) @ w_out


def swiglu(h, w_gu, w_down):
    z = h @ w_gu
    return (jax.nn.silu(z[..., :FFN_HIDDEN]) * z[..., FFN_HIDDEN:]) @ w_down


def setup_inputs(seed: int = 0) -> dict:
    key = jax.random.key(seed)
    ks = jax.random.split(key, 24)

    def nrm(k, shape, scale):
        return jax.random.normal(k, shape, f32) * scale

    return {
        'x': nrm(ks[0], (BATCH, SEQ, D_MODEL), 1.0),
        'c': nrm(ks[1], (BATCH, D_MODEL), 1.0),
        'ctx': nrm(ks[2], (BATCH, CTX_LEN, D_MODEL), 1.0),
        'c_ctx': nrm(ks[3], (D_MODEL,), 1.0),
        'w_ada': nrm(ks[4], (DEPTH, D_MODEL, 6 * D_MODEL), 0.5 * D_MODEL ** -0.5),
        'b_ada': nrm(ks[5], (DEPTH, 6 * D_MODEL), 0.02),
        'w_in': nrm(ks[6], (DEPTH, D_MODEL, IN_WIDTH), D_MODEL ** -0.5),
        'lam_qk': nrm(ks[7], (DEPTH, 4, A_DH), LAMBDA_STD),
        'subln_g': 1.0 + nrm(ks[8], (DEPTH, A_DV), 0.1),
        'gmlp_ln_g': 1.0 + nrm(ks[9], (DEPTH, BRANCH_WIDTH), 0.1),
        'gmlp_ln_b': nrm(ks[10], (DEPTH, BRANCH_WIDTH), 0.02),
        'w_spatial': nrm(ks[11], (DEPTH, B_GROUPS, B_CHUNK, B_CHUNK), B_CHUNK ** -0.5),
        'b_spatial': 1.0 + nrm(ks[12], (DEPTH, B_GROUPS, B_CHUNK), 0.1),
        'w_pool': nrm(ks[13], (DEPTH, C_GROUPS, C_GDIM, C_GDIM), C_GDIM ** -0.5),
        'pool_scale': 1.0 + nrm(ks[14], (DEPTH, BRANCH_WIDTH), 0.1),
        'w_branch': nrm(ks[15], (DEPTH, N_BRANCH, BRANCH_WIDTH, D_MODEL), BRANCH_WIDTH ** -0.5),
        'w_out': nrm(ks[16], (DEPTH, D_MODEL, D_MODEL), BETA * D_MODEL ** -0.5),
        'ln1_g': 1.0 + nrm(ks[17], (DEPTH, D_MODEL), 0.1),
        'ln1_b': nrm(ks[18], (DEPTH, D_MODEL), 0.02),
        'w_gu': nrm(ks[19], (DEPTH, D_MODEL, 2 * FFN_HIDDEN), D_MODEL ** -0.5),
        'w_down': nrm(ks[20], (DEPTH, FFN_HIDDEN, D_MODEL), BETA * FFN_HIDDEN ** -0.5),
        'ln2_g': 1.0 + nrm(ks[21], (DEPTH, D_MODEL), 0.1),
        'ln2_b': nrm(ks[22], (DEPTH, D_MODEL), 0.02),
    }


def reference(x, c, ctx, c_ctx, w_ada, b_ada, w_in, lam_qk, subln_g, gmlp_ln_g, gmlp_ln_b,
              w_spatial, b_spatial, w_pool, pool_scale, w_branch, w_out, ln1_g, ln1_b,
              w_gu, w_down, ln2_g, ln2_b):
    n_lat = x.shape[1]
    rows = n_lat // GRID_W
    row = jnp.repeat(jnp.arange(rows), GRID_W)
    col = jnp.tile(jnp.arange(GRID_W), rows)
    cos, sin = axial_rope_tables(row, col)

    x = norm_only(x)
    ctx = norm_only(ctx)

    for l in range(DEPTH):
        last = l == DEPTH - 1
        lam_init = 0.8 - 0.6 * math.exp(-0.3 * l)
        lq = lam_qk[l].astype(f32)
        lam = jnp.exp(jnp.sum(lq[0] * lq[1])) - jnp.exp(jnp.sum(lq[2] * lq[3])) + lam_init

        sh_m, sc_m, g_m, sh_f, sc_f, g_f = jnp.split(adaln(c, w_ada[l], b_ada[l], 6)[:, None, :], 6, axis=-1)
        n_ctx_mod = 2 if last else 6
        mods_c = jnp.split(adaln(c_ctx, w_ada[l], b_ada[l], n_ctx_mod), n_ctx_mod, axis=-1)

        h = x * (1.0 + sc_m) + sh_m
        hc = ctx * (1.0 + mods_c[1]) + mods_c[0]
        z = h @ w_in[l]
        if last:
            zc_kv = hc @ w_in[l][:, K_OFF:V_END]
        else:
            zc = hc @ w_in[l]
            zc_kv = zc[..., K_OFF:V_END]
        k_c = heads_qk(zc_kv[..., :A_QK_WIDTH])
        v_c = heads_v(zc_kv[..., A_QK_WIDTH:])

        q = apply_rope(heads_qk(z[..., Q_OFF:K_OFF]), cos, sin)
        k = apply_rope(heads_qk(z[..., K_OFF:V_OFF]), cos, sin)
        v = heads_v(z[..., V_OFF:V_END])
        k_all = jnp.concatenate([k_c, k], axis=1)
        v_all = jnp.concatenate([v_c, v], axis=1)
        y_a = diff_post(latent_diff_attention(q, k_all, v_all, lam), subln_g[l], lam_init)
        y_b = gmlp_branch(z[..., BU_OFF:C_OFF], gmlp_ln_g[l], gmlp_ln_b[l], w_spatial[l], b_spatial[l])
        y_c = pool_branch(z[..., C_OFF:G_OFF], w_pool[l], pool_scale[l])
        out = merge_branches(z[..., G_OFF:], y_a, y_b, y_c, w_branch[l], w_out[l])
        x = layer_norm(ALPHA * x + g_m * out, ln1_g[l], ln1_b[l])

        hf = x * (1.0 + sc_f) + sh_f
        x = layer_norm(ALPHA * x + g_f * swiglu(hf, w_gu[l], w_down[l]), ln2_g[l], ln2_b[l])

        if not last:
            _, _, g_mc, sh_fc, sc_fc, g_fc = mods_c
            q_c = heads_qk(zc[..., Q_OFF:K_OFF])
            y_ac = diff_post(diff_attention(q_c, k_c, v_c, lam), subln_g[l], lam_init)
            y_bc = gmlp_branch(zc[..., BU_OFF:C_OFF], gmlp_ln_g[l], gmlp_ln_b[l], w_spatial[l], b_spatial[l])
            y_cc = pool_branch(zc[..., C_OFF:G_OFF], w_pool[l], pool_scale[l])
            out_c = merge_branches(zc[..., G_OFF:], y_ac, y_bc, y_cc, w_branch[l], w_out[l])
            ctx = layer_norm(ALPHA * ctx + g_mc * out_c, ln1_g[l], ln1_b[l])
            hfc = ctx * (1.0 + sc_fc) + sh_fc
            ctx = layer_norm(ALPHA * ctx + g_fc * swiglu(hfc, w_gu[l], w_down[l]), ln2_g[l], ln2_b[l])

    return x
```

```python
import functools
import math
from typing import NamedTuple

import jax
import jax.numpy as jnp
from jax import lax
from jax.experimental import pallas as pl
from jax.experimental.pallas import tpu as pltpu

f32 = jnp.float32
bf16 = jnp.bfloat16

GRID_W = 64
A_DH = 64
A_DV = 2 * A_DH
ROPE_THETA = 10000.0
ROPE_PAIRS = A_DH // 4
A_SCALE = A_DH ** -0.5
B_CHUNK = 128
B_GROUPS = 8
C_WINDOWS = (2, 4, 8, 16)
C_HALO = 8
LN_EPS = 1e-6
N_MOD = 6
MOD_ROWS = 8
LOG2E = math.log2(math.e)

LANES = 128
SUBLANES = 8
VMEM_LIMIT_BYTES = 56 * 1024 * 1024


class Cfg(NamedTuple):
    B: int
    S: int
    CTX: int
    D: int
    DEPTH: int
    BW: int
    H: int
    FH: int
    nlat: int
    nctx: int
    ntok: int
    tm: int
    tr: int
    tq: int
    kchunk: int
    tf: int


def _pick(cands, *divs):
    for t in cands:
        if all(d % t == 0 for d in divs):
            return t
    raise ValueError(f"no tile in {cands} divides {divs}")


def _make_cfg(x, ctx, w_ada, w_gu):
    B, S, D = x.shape
    CTX = ctx.shape[1]
    BW = D // 2
    FH = w_gu.shape[2] // 2
    nlat, nctx = B * S, B * ctx.shape[1]
    assert B + 1 <= MOD_ROWS and S % GRID_W == 0 and nlat % CTX == 0
    assert CTX % B_CHUNK == 0 and S % B_CHUNK == 0 and BW % (4 * LANES) == 0
    tm = _pick((1024, 512, 256, 128), S, nctx)
    tr = _pick((512, 256, 128), S, nctx)
    tq = _pick((256, 128), S, CTX)
    kchunk = _pick((512, 256, 128), S)
    tf = _pick((512, 256, 128), FH)
    return Cfg(B, S, CTX, D, w_ada.shape[0], BW, BW // A_DV, FH, nlat, nctx, nlat + nctx,
               tm, tr, tq, kchunk, tf)


def _params(*sem):
    return pltpu.CompilerParams(dimension_semantics=sem, vmem_limit_bytes=VMEM_LIMIT_BYTES)


def _mod_spec(cfg, layer, chunk, tile):
    nlat_tiles = cfg.nlat // tile

    def idx(i, *_):
        r = jnp.where(i < nlat_tiles, (i * tile) // cfg.S, cfg.B)
        return ((layer * MOD_ROWS + r) * N_MOD + chunk, 0, 0)

    return pl.BlockSpec((None, 1, cfg.D), idx)


def _layer_norm(xf, g, b):
    mu = jnp.mean(xf, -1, keepdims=True)
    xc = xf - mu
    var = jnp.mean(xc * xc, -1, keepdims=True)
    return xc * lax.rsqrt(var + LN_EPS) * g + b


def _ada_kernel(c_ref, w_ref, b_ref, o_ref):
    cv = c_ref[...]
    a = (cv * jax.nn.sigmoid(cv)).astype(bf16)
    o_ref[...] = jnp.dot(a, w_ref[...].astype(bf16), preferred_element_type=f32) + b_ref[...]


def _ada_call(cfg, cond, w_ada, b_ada):
    width = N_MOD * cfg.D
    tn = _pick((1024, 512, 256, 128), width)
    return pl.pallas_call(
        _ada_kernel,
        grid=(cfg.DEPTH, width // tn),
        in_specs=[
            pl.BlockSpec((MOD_ROWS, cfg.D), lambda l, j: (0, 0)),
            pl.BlockSpec((None, cfg.D, tn), lambda l, j: (l, 0, j)),
            pl.BlockSpec((None, 1, tn), lambda l, j: (l, 0, j)),
        ],
        out_specs=pl.BlockSpec((None, MOD_ROWS, tn), lambda l, j: (l, 0, j)),
        out_shape=jax.ShapeDtypeStruct((cfg.DEPTH, MOD_ROWS, width), f32),
        compiler_params=_params("parallel", "parallel"),
        name="adaln_table",
    )(cond, w_ada, b_ada.reshape(cfg.DEPTH, 1, width))


def _entry_kernel(nlat_tiles, x_ref, c_ref, sh_ref, sc_ref, xo_ref, ho_ref):
    def body(src_ref):
        xf = src_ref[...]
        mu = jnp.mean(xf, -1, keepdims=True)
        xc = xf - mu
        var = jnp.mean(xc * xc, -1, keepdims=True)
        xn = xc * lax.rsqrt(var + LN_EPS)
        xo_ref[...] = xn
        ho_ref[...] = (xn * (1.0 + sc_ref[...]) + sh_ref[...]).astype(bf16)

    i = pl.program_id(0)
    pl.when(i < nlat_tiles)(lambda: body(x_ref))
    pl.when(i >= nlat_tiles)(lambda: body(c_ref))


def _entry_call(cfg, x2, c2, mods):
    t = cfg.tr
    nlat_tiles = cfg.nlat // t
    row = lambda i: (i, 0)
    return pl.pallas_call(
        functools.partial(_entry_kernel, nlat_tiles),
        grid=(cfg.ntok // t,),
        in_specs=[
            pl.BlockSpec((t, cfg.D), lambda i: (jnp.minimum(i, nlat_tiles - 1), 0)),
            pl.BlockSpec((t, cfg.D), lambda i: (jnp.maximum(i - nlat_tiles, 0), 0)),
            _mod_spec(cfg, 0, 0, t),
            _mod_spec(cfg, 0, 1, t),
        ],
        out_specs=[pl.BlockSpec((t, cfg.D), row), pl.BlockSpec((t, cfg.D), row)],
        out_shape=[jax.ShapeDtypeStruct((cfg.ntok, cfg.D), f32),
                   jax.ShapeDtypeStruct((cfg.ntok, cfg.D), bf16)],
        compiler_params=_params("parallel"),
        name="entry_norm",
    )(x2, c2, mods, mods)


def _inproj_kernel(kind, h_ref, w_ref, *refs):
    acc = jnp.dot(h_ref[...], w_ref[...], preferred_element_type=f32)
    o_ref = refs[-1]
    if kind == "rope":
        cos_ref, sin_ref = refs[0], refs[1]
        cs, sn = cos_ref[...], sin_ref[...]
        lane = lax.broadcasted_iota(jnp.int32, cs.shape, 1)
        first_half = (lane % (2 * ROPE_PAIRS)) < ROPE_PAIRS
        for cb in range(acc.shape[1] // LANES):
            cols = slice(cb * LANES, (cb + 1) * LANES)
            z = acc[:, cols]
            sw = jnp.where(first_half, pltpu.roll(z, LANES - ROPE_PAIRS, 1), pltpu.roll(z, ROPE_PAIRS, 1))
            o_ref[:, cols] = (z * cs + sw * sn).astype(o_ref.dtype)
    elif kind == "gelu":
        o_ref[...] = (0.5 * acc * (1.0 + lax.erf(acc * (2.0 ** -0.5)))).astype(o_ref.dtype)
    elif kind == "sigmoid":
        o_ref[...] = jax.nn.sigmoid(acc).astype(o_ref.dtype)
    else:
        o_ref[...] = acc.astype(o_ref.dtype)


def _inproj_call(cfg, h, w, layer, col0, ncols, nrows, kind, out_dtype, tables=None):
    tm = cfg.tm
    tn = _pick((1024, 512, 256, 128), ncols, col0)
    j0 = col0 // tn
    in_specs = [
        pl.BlockSpec((tm, cfg.D), lambda i, j: (i, 0)),
        pl.BlockSpec((None, cfg.D, tn), lambda i, j: (layer, 0, j0 + j)),
    ]
    args = [h, w]
    if kind == "rope":
        nlat_tiles, per_seq = cfg.nlat // tm, cfg.S // tm

        def tab_idx(i, j):
            return (j * tn // cfg.BW, jnp.where(i < nlat_tiles, i % per_seq, per_seq), 0)

        in_specs += [pl.BlockSpec((None, tm, LANES), tab_idx)] * 2
        args += list(tables)
    return pl.pallas_call(
        functools.partial(_inproj_kernel, kind),
        grid=(nrows // tm, ncols // tn),
        in_specs=in_specs,
        out_specs=pl.BlockSpec((tm, tn), lambda i, j: (i, j)),
        out_shape=jax.ShapeDtypeStruct((nrows, ncols), out_dtype),
        compiler_params=_params("parallel", "parallel"),
        name="inproj_" + kind,
    )(*args)


def _rope_tables(cfg):
    t = jnp.arange(cfg.S)
    row, col = (t // GRID_W).astype(f32), (t % GRID_W).astype(f32)
    inv = ROPE_THETA ** (-jnp.arange(ROPE_PAIRS, dtype=f32) / ROPE_PAIRS)
    lane = jnp.arange(LANES)
    j = lane % A_DH
    axis, half, pair = j // (2 * ROPE_PAIRS), (j % (2 * ROPE_PAIRS)) // ROPE_PAIRS, j % ROPE_PAIRS
    ang = jnp.where(axis[None, :] == 0, row[:, None], col[:, None]) * inv[pair][None, :]
    cos = jnp.cos(ang)
    sin = jnp.sin(ang) * jnp.where(half == 0, -1.0, 1.0)[None, :]
    cos = jnp.concatenate([cos, jnp.ones((cfg.tm, LANES), f32)], 0)
    sin = jnp.concatenate([sin, jnp.zeros((cfg.tm, LANES), f32)], 0)
    qs = A_SCALE * LOG2E
    return jnp.stack([cos * qs, cos]), jnp.stack([sin * qs, sin])


_NT = (((1,), (1,)), ((), ()))


def _attn_kernel(tq, n_ctx, n_lat, kchunk, lam_init, lq_ref, g_ref, q_ref, kc_ref, vc_ref, *rest):
    if n_lat:
        kl_ref, vl_ref, o_ref, s_scr, vaug = rest
    else:
        o_ref, s_scr, vaug = rest
    nk = n_ctx + n_lat

    @pl.when(pl.program_id(2) == 0)
    def _():
        vaug[0:n_ctx, 0:A_DV] = vc_ref[...]
        if n_lat:
            vaug[n_ctx:nk, 0:A_DV] = vl_ref[...]
        vaug[:, A_DV:2 * A_DV] = jnp.ones((nk, A_DV), bf16)

    q = q_ref[...]
    lane = lax.broadcasted_iota(jnp.int32, q.shape, 1)
    zero = jnp.zeros_like(q)
    qq = jnp.concatenate([jnp.where(lane < A_DH, q, zero), jnp.where(lane >= A_DH, q, zero)], axis=0)

    chunks = [(kc_ref, 0, n_ctx, 0)]
    chunks += [(kl_ref, c * kchunk, kchunk, n_ctx + c * kchunk) for c in range(n_lat // kchunk)]

    mx = None
    for ref, r0, n, c0 in chunks:
        s = lax.dot_general(qq, ref[r0:r0 + n, :], _NT, preferred_element_type=f32)
        s_scr[:, c0:c0 + n] = s
        for cb in range(n // LANES):
            blk = s[:, cb * LANES:(cb + 1) * LANES]
            mx = blk if mx is None else jnp.maximum(mx, blk)
    m = jnp.max(mx, axis=1, keepdims=True)

    pv = None
    for ref, r0, n, c0 in chunks:
        e = jnp.exp2(s_scr[:, c0:c0 + n] - m).astype(bf16)
        d = jnp.dot(e, vaug[c0:c0 + n, :], preferred_element_type=f32)
        pv = d if pv is None else pv + d

    o = pv[:, 0:A_DV] / pv[:, A_DV:A_DV + 1]
    lq = lq_ref[...]
    lam = (jnp.exp(jnp.sum(lq[0:1] * lq[1:2], axis=1, keepdims=True))
           - jnp.exp(jnp.sum(lq[2:3] * lq[3:4], axis=1, keepdims=True)) + lam_init)
    out = o[0:tq] - lam * o[tq:2 * tq]
    ms = jnp.mean(out * out, -1, keepdims=True)
    y = out * lax.rsqrt(ms + LN_EPS) * g_ref[...] * (1.0 - lam_init)
    o_ref[...] = y.astype(o_ref.dtype)


def _attn_call(cfg, qk, v, lam_qk_l, subln_g_l, lam_init, latent):
    H = cfg.H
    ctx_blk0 = cfg.nlat // cfg.CTX
    if latent:
        tq, nq, n_lat = cfg.tq, cfg.S // cfg.tq, cfg.S
        q_idx = lambda b, h, i: (b * nq + i, h)
        out_rows = cfg.nlat
    else:
        tq, nq, n_lat = cfg.CTX, 1, 0
        q_idx = lambda b, h, i: (ctx_blk0 + b, h)
        out_rows = cfg.nctx
    o_idx = lambda b, h, i: (b * nq + i, h)
    in_specs = [
        pl.BlockSpec((4, A_DH), lambda b, h, i: (0, 0)),
        pl.BlockSpec((1, A_DV), lambda b, h, i: (0, 0)),
        pl.BlockSpec((tq, A_DV), q_idx),
        pl.BlockSpec((cfg.CTX, A_DV), lambda b, h, i: (ctx_blk0 + b, H + h)),
        pl.BlockSpec((cfg.CTX, A_DV), lambda b, h, i: (ctx_blk0 + b, h)),
    ]
    args = [lam_qk_l, subln_g_l.reshape(1, A_DV), qk, qk, v]
    if latent:
        in_specs += [
            pl.BlockSpec((cfg.S, A_DV), lambda b, h, i: (b, H + h)),
            pl.BlockSpec((cfg.S, A_DV), lambda b, h, i: (b, h)),
        ]
        args += [qk, v]
    nk = cfg.CTX + n_lat
    return pl.pallas_call(
        functools.partial(_attn_kernel, tq, cfg.CTX, n_lat, cfg.kchunk, lam_init),
        grid=(cfg.B, H, nq),
        in_specs=in_specs,
        out_specs=pl.BlockSpec((tq, A_DV), o_idx),
        out_shape=jax.ShapeDtypeStruct((out_rows, cfg.BW), bf16),
        scratch_shapes=[pltpu.VMEM((2 * tq, nk), f32), pltpu.VMEM((nk, 2 * A_DV), bf16)],
        compiler_params=_params("parallel", "parallel", "arbitrary"),
        name="diff_attn_lat" if latent else "diff_attn_ctx",
    )(*args)


def _gmlp_kernel(u_ref, v_ref, g_ref, b_ref, ws_ref, bs_ref, o_ref):
    vn = _layer_norm(v_ref[...], g_ref[...], b_ref[...]).astype(bf16)
    nch = vn.shape[0] // B_CHUNK
    gdim = vn.shape[1] // B_GROUPS
    for g in range(B_GROUPS):
        cols = slice(g * gdim, (g + 1) * gdim)
        vg = jnp.concatenate([vn[n * B_CHUNK:(n + 1) * B_CHUNK, cols] for n in range(nch)], axis=1)
        s = jnp.dot(ws_ref[g], vg, preferred_element_type=f32) + bs_ref[:, g:g + 1]
        for n in range(nch):
            rows = slice(n * B_CHUNK, (n + 1) * B_CHUNK)
            o_ref[rows, cols] = (u_ref[rows, cols] * s[:, n * gdim:(n + 1) * gdim]).astype(o_ref.dtype)


def _gmlp_call(cfg, zuv, nrows, ln_g, ln_b, ws_l, bs_t):
    t = cfg.tq
    return pl.pallas_call(
        _gmlp_kernel,
        grid=(nrows // t,),
        in_specs=[
            pl.BlockSpec((t, cfg.BW), lambda i: (i, 0)),
            pl.BlockSpec((t, cfg.BW), lambda i: (i, 1)),
            pl.BlockSpec((1, cfg.BW), lambda i: (0, 0)),
            pl.BlockSpec((1, cfg.BW), lambda i: (0, 0)),
            pl.BlockSpec((B_GROUPS, B_CHUNK, B_CHUNK), lambda i: (0, 0, 0)),
            pl.BlockSpec((B_CHUNK, B_GROUPS), lambda i: (0, 0)),
        ],
        out_specs=pl.BlockSpec((t, cfg.BW), lambda i: (i, 0)),
        out_shape=jax.ShapeDtypeStruct((nrows, cfg.BW), bf16),
        compiler_params=_params("parallel"),
        name="gmlp",
    )(zuv, zuv, ln_g.reshape(1, cfg.BW), ln_b.reshape(1, cfg.BW), ws_l, bs_t)


def _pool_kernel(cfg, t, cur_ref, prev_ref, next_ref, wp_ref, sc_ref, o_ref, ext):
    r0 = pl.program_id(0) * t
    is_lat = r0 < cfg.nlat
    n = jnp.where(is_lat, cfg.S, cfg.CTX)
    pos0 = jnp.where(is_lat, r0 % cfg.S, r0 % cfg.CTX)
    halo = lax.broadcasted_iota(jnp.int32, (C_HALO, 1), 0)
    ext[0:C_HALO, :] = jnp.where(pos0 - C_HALO + halo >= 0, prev_ref[...], 0.0)
    ext[C_HALO:C_HALO + t, :] = cur_ref[...]
    ext[C_HALO + t:2 * C_HALO + t, :] = jnp.where(pos0 + t + halo < n, next_ref[...], 0.0)
    pos = pos0 + lax.broadcasted_iota(jnp.int32, (t, 1), 0)
    gdim = cfg.BW // len(C_WINDOWS)
    for g, w in enumerate(C_WINDOWS):
        cols = slice(g * gdim, (g + 1) * gdim)
        acc = ext[C_HALO - w // 2:C_HALO - w // 2 + t, cols]
        for o in range(-w // 2 + 1, w // 2):
            acc = acc + ext[C_HALO + o:C_HALO + o + t, cols]
        lo = jnp.maximum(pos - w // 2, 0)
        hi = jnp.minimum(pos - w // 2 + w, n)
        inv = 1.0 / (hi - lo).astype(f32)
        d = (acc * inv - cur_ref[:, cols]).astype(bf16)
        y = jnp.dot(d, wp_ref[g], preferred_element_type=f32) * sc_ref[:, cols]
        o_ref[:, cols] = y.astype(o_ref.dtype)


def _pool_call(cfg, zc, nrows, wp_l, scale_l):
    t = cfg.tq
    hb = t // C_HALO
    last_hb = zc.shape[0] // C_HALO - 1
    gdim = cfg.BW // len(C_WINDOWS)
    return pl.pallas_call(
        functools.partial(_pool_kernel, cfg, t),
        grid=(nrows // t,),
        in_specs=[
            pl.BlockSpec((t, cfg.BW), lambda i: (i, 0)),
            pl.BlockSpec((C_HALO, cfg.BW), lambda i: (jnp.maximum(i * hb - 1, 0), 0)),
            pl.BlockSpec((C_HALO, cfg.BW), lambda i: (jnp.minimum((i + 1) * hb, last_hb), 0)),
            pl.BlockSpec((len(C_WINDOWS), gdim, gdim), lambda i: (0, 0, 0)),
            pl.BlockSpec((1, cfg.BW), lambda i: (0, 0)),
        ],
        out_specs=pl.BlockSpec((t, cfg.BW), lambda i: (i, 0)),
        out_shape=jax.ShapeDtypeStruct((nrows, cfg.BW), bf16),
        scratch_shapes=[pltpu.VMEM((t + 2 * C_HALO, cfg.BW), f32)],
        compiler_params=_params("parallel"),
        name="pool",
    )(zc, zc, zc, wp_l, scale_l.reshape(1, cfg.BW))


def _merge_kernel(ya_ref, yb_ref, yc_ref, ga_ref, gb_ref, gc_ref, wb_ref, o_ref):
    acc = ga_ref[...] * jnp.dot(ya_ref[...], wb_ref[0], preferred_element_type=f32)
    acc = acc + gb_ref[...] * jnp.dot(yb_ref[...], wb_ref[1], preferred_element_type=f32)
    acc = acc + gc_ref[...] * jnp.dot(yc_ref[...], wb_ref[2], preferred_element_type=f32)
    o_ref[...] = acc.astype(o_ref.dtype)


def _merge_call(cfg, ya, yb, yc, gates, nrows, wb, layer):
    t = cfg.tr
    tn = _pick((1024, 512, 256, 128), cfg.D)
    nj = cfg.D // tn
    y_spec = pl.BlockSpec((t, cfg.BW), lambda i, j: (i, 0))
    g_specs = [pl.BlockSpec((t, tn), functools.partial(lambda i, j, n: (i, n * nj + j), n=n)) for n in range(3)]
    return pl.pallas_call(
        _merge_kernel,
        grid=(nrows // t, nj),
        in_specs=[y_spec, y_spec, y_spec, *g_specs,
                  pl.BlockSpec((None, 3, cfg.BW, tn), lambda i, j: (layer, 0, 0, j))],
        out_specs=pl.BlockSpec((t, tn), lambda i, j: (i, j)),
        out_shape=jax.ShapeDtypeStruct((nrows, cfg.D), bf16),
        compiler_params=_params("parallel", "parallel"),
        name="merge",
    )(ya, yb, yc, gates, gates, gates, wb)


def _outproj_kernel(alpha, m_ref, w_ref, x_ref, gm_ref, shf_ref, scf_ref, lg_ref, lb_ref, xo_ref, ho_ref):
    out = jnp.dot(m_ref[...], w_ref[...], preferred_element_type=f32)
    xn = _layer_norm(alpha * x_ref[...] + gm_ref[...] * out, lg_ref[...], lb_ref[...])
    xo_ref[...] = xn
    ho_ref[...] = (xn * (1.0 + scf_ref[...]) + shf_ref[...]).astype(ho_ref.dtype)


def _outproj_call(cfg, m, xs, nrows, w_out, mods, layer, alpha, ln_g, ln_b):
    t = cfg.tr
    row = lambda i: (i, 0)
    vec = pl.BlockSpec((1, cfg.D), lambda i: (0, 0))
    return pl.pallas_call(
        functools.partial(_outproj_kernel, alpha),
        grid=(nrows // t,),
        in_specs=[
            pl.BlockSpec((t, cfg.D), row),
            pl.BlockSpec((None, cfg.D, cfg.D), lambda i: (layer, 0, 0)),
            pl.BlockSpec((t, cfg.D), row),
            _mod_spec(cfg, layer, 2, t), _mod_spec(cfg, layer, 3, t), _mod_spec(cfg, layer, 4, t),
            vec, vec,
        ],
        out_specs=[pl.BlockSpec((t, cfg.D), row), pl.BlockSpec((t, cfg.D), row)],
        out_shape=[jax.ShapeDtypeStruct((nrows, cfg.D), f32), jax.ShapeDtypeStruct((nrows, cfg.D), bf16)],
        compiler_params=_params("parallel"),
        name="outproj_ln",
    )(m, w_out, xs, mods, mods, mods, ln_g.reshape(1, cfg.D), ln_b.reshape(1, cfg.D))


def _ffn_up_kernel(h_ref, wg_ref, wu_ref, o_ref):
    h = h_ref[...]
    g = jnp.dot(h, wg_ref[...], preferred_element_type=f32)
    u = jnp.dot(h, wu_ref[...], preferred_element_type=f32)
    o_ref[...] = (g * jax.nn.sigmoid(g) * u).astype(o_ref.dtype)


def _ffn_up_call(cfg, hf, nrows, w_gu, layer):
    tm, tf = cfg.tm, cfg.tf
    nj = cfg.FH // tf
    return pl.pallas_call(
        _ffn_up_kernel,
        grid=(nrows // tm, nj),
        in_specs=[
            pl.BlockSpec((tm, cfg.D), lambda i, j: (i, 0)),
            pl.BlockSpec((None, cfg.D, tf), lambda i, j: (layer, 0, j)),
            pl.BlockSpec((None, cfg.D, tf), lambda i, j: (layer, 0, nj + j)),
        ],
        out_specs=pl.BlockSpec((tm, tf), lambda i, j: (i, j)),
        out_shape=jax.ShapeDtypeStruct((nrows, cfg.FH), bf16),
        compiler_params=_params("parallel", "parallel"),
        name="ffn_up",
    )(hf, w_gu, w_gu)


def _ffn_down_kernel(alpha, nk, has_next, act_ref, w_ref, x_ref, gf_ref, lg_ref, lb_ref, *rest):
    if has_next:
        shn_ref, scn_ref, xo_ref, ho_ref, acc = rest
    else:
        xo_ref, acc = rest
    k = pl.program_id(1)

    @pl.when(k == 0)
    def _():
        acc[...] = jnp.zeros_like(acc)

    acc[...] += jnp.dot(act_ref[...], w_ref[...], preferred_element_type=f32)

    @pl.when(k == nk - 1)
    def _():
        xn = _layer_norm(alpha * x_ref[...] + gf_ref[...] * acc[...], lg_ref[...], lb_ref[...])
        xo_ref[...] = xn
        if has_next:
            ho_ref[...] = (xn * (1.0 + scn_ref[...]) + shn_ref[...]).astype(ho_ref.dtype)


def _ffn_down_call(cfg, act, xs, nrows, w_down, mods, layer, alpha, ln_g, ln_b):
    t, tk = cfg.tr, cfg.tf
    nk = cfg.FH // tk
    has_next = layer + 1 < cfg.DEPTH
    row = lambda i, k: (i, 0)
    vec = pl.BlockSpec((1, cfg.D), lambda i, k: (0, 0))
    in_specs = [
        pl.BlockSpec((t, tk), lambda i, k: (i, k)),
        pl.BlockSpec((None, tk, cfg.D), lambda i, k: (layer, k, 0)),
        pl.BlockSpec((t, cfg.D), row),
        _mod_spec(cfg, layer, 5, t), vec, vec,
    ]
    args = [act, w_down, xs, mods, ln_g.reshape(1, cfg.D), ln_b.reshape(1, cfg.D)]
    out_specs = [pl.BlockSpec((t, cfg.D), row)]
    out_shape = [jax.ShapeDtypeStruct((nrows, cfg.D), f32)]
    if has_next:
        in_specs += [_mod_spec(cfg, layer + 1, 0, t), _mod_spec(cfg, layer + 1, 1, t)]
        args += [mods, mods]
        out_specs.append(pl.BlockSpec((t, cfg.D), row))
        out_shape.append(jax.ShapeDtypeStruct((nrows, cfg.D), bf16))
    return pl.pallas_call(
        functools.partial(_ffn_down_kernel, alpha, nk, has_next),
        grid=(nrows // t, nk),
        in_specs=in_specs,
        out_specs=out_specs,
        out_shape=out_shape,
        scratch_shapes=[pltpu.VMEM((t, cfg.D), f32)],
        compiler_params=_params("parallel", "arbitrary"),
        name="ffn_down_ln",
    )(*args)


def kernel(x, c, ctx, c_ctx, w_ada, b_ada, w_in, lam_qk, subln_g, gmlp_ln_g, gmlp_ln_b,
           w_spatial, b_spatial, w_pool, pool_scale, w_branch, w_out, ln1_g, ln1_b,
           w_gu, w_down, ln2_g, ln2_b):
    cfg = _make_cfg(x, ctx, w_ada, w_gu)
    D, BW = cfg.D, cfg.BW
    alpha = (2 * cfg.DEPTH) ** 0.25

    cond = jnp.zeros((MOD_ROWS, D), f32).at[:cfg.B].set(c).at[cfg.B].set(c_ctx)
    mods = _ada_call(cfg, cond, w_ada, b_ada).reshape(cfg.DEPTH * MOD_ROWS * N_MOD, 1, D)
    tables = _rope_tables(cfg)

    w_in_b, w_sp_b, w_pool_b = w_in.astype(bf16), w_spatial.astype(bf16), w_pool.astype(bf16)
    w_br_b, w_out_b = w_branch.astype(bf16), w_out.astype(bf16)
    w_gu_b, w_down_b = w_gu.astype(bf16), w_down.astype(bf16)

    xs, h = _entry_call(cfg, x.reshape(cfg.nlat, D), ctx.reshape(cfg.nctx, D), mods)

    for l in range(cfg.DEPTH):
        last = l == cfg.DEPTH - 1
        nrows = cfg.nlat if last else cfg.ntok
        lam_init = 0.8 - 0.6 * math.exp(-0.3 * l)

        qk = _inproj_call(cfg, h, w_in_b, l, 0, 2 * BW, cfg.ntok, "rope", bf16, tables)
        v = _inproj_call(cfg, h, w_in_b, l, 2 * BW, BW, cfg.ntok, "cast", bf16)
        zuv = _inproj_call(cfg, h, w_in_b, l, 3 * BW, 2 * BW, nrows, "gelu", f32)
        zc = _inproj_call(cfg, h, w_in_b, l, 5 * BW, BW, nrows, "copy", f32)
        gates = _inproj_call(cfg, h, w_in_b, l, 6 * BW, 3 * D, nrows, "sigmoid", f32)

        ya = _attn_call(cfg, qk, v, lam_qk[l], subln_g[l], lam_init, latent=True)
        if not last:
            ya = jnp.concatenate([ya, _attn_call(cfg, qk, v, lam_qk[l], subln_g[l], lam_init, latent=False)], 0)
        yb = _gmlp_call(cfg, zuv, nrows, gmlp_ln_g[l], gmlp_ln_b[l], w_sp_b[l], b_spatial[l].T)
        yc = _pool_call(cfg, zc, nrows, w_pool_b[l], pool_scale[l])

        m = _merge_call(cfg, ya, yb, yc, gates, nrows, w_br_b, l)
        xs, hf = _outproj_call(cfg, m, xs, nrows, w_out_b, mods, l, alpha, ln1_g[l], ln1_b[l])
        act = _ffn_up_call(cfg, hf, nrows, w_gu_b, l)
        res = _ffn_down_call(cfg, act, xs, nrows, w_down_b, mods, l, alpha, ln2_g[l], ln2_b[l])
        if last:
            xs = res[0]
        else:
            xs, h = res

    return xs.reshape(cfg.B, cfg.S, D)
```

```python
import functools
import math
from typing import NamedTuple

import jax
import jax.numpy as jnp
from jax import lax
from jax.experimental import pallas as pl
from jax.experimental.pallas import tpu as pltpu

f32 = jnp.float32
bf16 = jnp.bfloat16

GRID_W = 64
A_DH = 64
A_DV = 2 * A_DH
ROPE_THETA = 10000.0
ROPE_PAIRS = A_DH // 4
A_SCALE = A_DH ** -0.5
B_CHUNK = 128
B_GROUPS = 8
C_WINDOWS = (2, 4, 8, 16)
C_HALO = 8
LN_EPS = 1e-6
N_MOD = 6
MOD_ROWS = 8
LOG2E = math.log2(math.e)

LANES = 128
SUBLANES = 8
VMEM_LIMIT_BYTES = 56 * 1024 * 1024


class Cfg(NamedTuple):
    B: int
    S: int
    CTX: int
    D: int
    DEPTH: int
    BW: int
    H: int
    FH: int
    nlat: int
    nctx: int
    ntok: int
    tm: int
    tr: int
    ts: int
    tq: int
    tql: int
    kchunk: int
    tf: int


def _pick(cands, *divs):
    for t in cands:
        if all(d % t == 0 for d in divs):
            return t
    raise ValueError(f"no tile in {cands} divides {divs}")


def _make_cfg(x, ctx, w_ada, w_gu):
    B, S, D = x.shape
    CTX = ctx.shape[1]
    BW = D // 2
    FH = w_gu.shape[2] // 2
    nlat, nctx = B * S, B * ctx.shape[1]
    assert B + 1 <= MOD_ROWS and S % GRID_W == 0 and nlat % CTX == 0
    assert CTX % B_CHUNK == 0 and S % B_CHUNK == 0 and BW % (4 * LANES) == 0
    tm = _pick((1024, 512, 256, 128), S, nctx)
    tr = _pick((512, 256, 128), S, nctx)
    ts = _pick((256, 128), S, nctx)
    tq = _pick((256, 128), S, CTX)
    tql = _pick((256, 128), S // 2)
    kchunk = _pick((512, 256, 128), S)
    tf = _pick((512, 256, 128), FH)
    return Cfg(B, S, CTX, D, w_ada.shape[0], BW, BW // A_DV, FH, nlat, nctx, nlat + nctx,
               tm, tr, ts, tq, tql, kchunk, tf)


def _params(*sem):
    return pltpu.CompilerParams(dimension_semantics=sem, vmem_limit_bytes=VMEM_LIMIT_BYTES)


def _mod_spec(cfg, layer, chunk, tile):
    nlat_tiles = cfg.nlat // tile

    def idx(i, *_):
        r = jnp.where(i < nlat_tiles, (i * tile) // cfg.S, cfg.B)
        return ((layer * MOD_ROWS + r) * N_MOD + chunk, 0, 0)

    return pl.BlockSpec((None, 1, cfg.D), idx)


def _layer_norm(xf, g, b):
    mu = jnp.mean(xf, -1, keepdims=True)
    xc = xf - mu
    var = jnp.mean(xc * xc, -1, keepdims=True)
    return xc * lax.rsqrt(var + LN_EPS) * g + b


def _ada_kernel(c_ref, w_ref, b_ref, o_ref):
    cv = c_ref[...]
    a = (cv * jax.nn.sigmoid(cv)).astype(bf16)
    o_ref[...] = jnp.dot(a, w_ref[...].astype(bf16), preferred_element_type=f32) + b_ref[...]


def _ada_call(cfg, cond, w_ada, b_ada):
    width = N_MOD * cfg.D
    tn = _pick((1024, 512, 256, 128), width)
    return pl.pallas_call(
        _ada_kernel,
        grid=(cfg.DEPTH, width // tn),
        in_specs=[
            pl.BlockSpec((MOD_ROWS, cfg.D), lambda l, j: (0, 0)),
            pl.BlockSpec((None, cfg.D, tn), lambda l, j: (l, 0, j)),
            pl.BlockSpec((None, 1, tn), lambda l, j: (l, 0, j)),
        ],
        out_specs=pl.BlockSpec((None, MOD_ROWS, tn), lambda l, j: (l, 0, j)),
        out_shape=jax.ShapeDtypeStruct((cfg.DEPTH, MOD_ROWS, width), f32),
        compiler_params=_params("parallel", "parallel"),
        name="adaln_table",
    )(cond, w_ada, b_ada.reshape(cfg.DEPTH, 1, width))


def _entry_kernel(nlat_tiles, x_ref, c_ref, sh_ref, sc_ref, xo_ref, ho_ref):
    def body(src_ref):
        xf = src_ref[...]
        mu = jnp.mean(xf, -1, keepdims=True)
        xc = xf - mu
        var = jnp.mean(xc * xc, -1, keepdims=True)
        xn = xc * lax.rsqrt(var + LN_EPS)
        xo_ref[...] = xn
        ho_ref[...] = (xn * (1.0 + sc_ref[...]) + sh_ref[...]).astype(bf16)

    i = pl.program_id(0)
    pl.when(i < nlat_tiles)(lambda: body(x_ref))
    pl.when(i >= nlat_tiles)(lambda: body(c_ref))


def _entry_call(cfg, x2, c2, mods):
    t = cfg.tr
    nlat_tiles = cfg.nlat // t
    row = lambda i: (i, 0)
    return pl.pallas_call(
        functools.partial(_entry_kernel, nlat_tiles),
        grid=(cfg.ntok // t,),
        in_specs=[
            pl.BlockSpec((t, cfg.D), lambda i: (jnp.minimum(i, nlat_tiles - 1), 0)),
            pl.BlockSpec((t, cfg.D), lambda i: (jnp.maximum(i - nlat_tiles, 0), 0)),
            _mod_spec(cfg, 0, 0, t),
            _mod_spec(cfg, 0, 1, t),
        ],
        out_specs=[pl.BlockSpec((t, cfg.D), row), pl.BlockSpec((t, cfg.D), row)],
        out_shape=[jax.ShapeDtypeStruct((cfg.ntok, cfg.D), f32),
                   jax.ShapeDtypeStruct((cfg.ntok, cfg.D), bf16)],
        compiler_params=_params("parallel"),
        name="entry_norm",
    )(x2, c2, mods, mods)


def _inproj_kernel(kind, h_ref, w_ref, *refs):
    o_ref, wb = refs[-2], refs[-1]

    @pl.when(pl.program_id(1) == 0)
    def _():
        wb[...] = w_ref[...].astype(bf16)

    acc = jnp.dot(h_ref[...], wb[...], preferred_element_type=f32)
    if kind == "rope":
        cos_ref, sin_ref = refs[0], refs[1]
        cs, sn = cos_ref[...], sin_ref[...]
        lane = lax.broadcasted_iota(jnp.int32, cs.shape, 1)
        first_half = (lane % (2 * ROPE_PAIRS)) < ROPE_PAIRS
        for cb in range(acc.shape[1] // LANES):
            cols = slice(cb * LANES, (cb + 1) * LANES)
            z = acc[:, cols]
            sw = jnp.where(first_half, pltpu.roll(z, LANES - ROPE_PAIRS, 1), pltpu.roll(z, ROPE_PAIRS, 1))
            o_ref[:, cols] = (z * cs + sw * sn).astype(o_ref.dtype)
    elif kind == "gelu":
        o_ref[...] = (0.5 * acc * (1.0 + lax.erf(acc * (2.0 ** -0.5)))).astype(o_ref.dtype)
    elif kind == "sigmoid":
        o_ref[...] = jax.nn.sigmoid(acc).astype(o_ref.dtype)
    else:
        o_ref[...] = acc.astype(o_ref.dtype)


def _inproj_call(cfg, h, w, layer, col0, ncols, nrows, kind, out_dtype, tables=None):
    tm = cfg.tm
    tn = _pick((1024, 512, 256, 128), ncols, col0)
    j0 = col0 // tn
    in_specs = [
        pl.BlockSpec((tm, cfg.D), lambda j, i: (i, 0)),
        pl.BlockSpec((None, cfg.D, tn), lambda j, i: (layer, 0, j0 + j)),
    ]
    args = [h, w]
    if kind == "rope":
        nlat_tiles, per_seq = cfg.nlat // tm, cfg.S // tm

        def tab_idx(j, i):
            return (j * tn // cfg.BW, jnp.where(i < nlat_tiles, i % per_seq, per_seq), 0)

        in_specs += [pl.BlockSpec((None, tm, LANES), tab_idx)] * 2
        args += list(tables)
    return pl.pallas_call(
        functools.partial(_inproj_kernel, kind),
        grid=(ncols // tn, nrows // tm),
        in_specs=in_specs,
        out_specs=pl.BlockSpec((tm, tn), lambda j, i: (i, j)),
        out_shape=jax.ShapeDtypeStruct((nrows, ncols), out_dtype),
        scratch_shapes=[pltpu.VMEM((cfg.D, tn), bf16)],
        compiler_params=_params("parallel", "arbitrary"),
        name="inproj_" + kind,
    )(*args)


def _rope_tables(cfg):
    t = jnp.arange(cfg.S)
    row, col = (t // GRID_W).astype(f32), (t % GRID_W).astype(f32)
    inv = ROPE_THETA ** (-jnp.arange(ROPE_PAIRS, dtype=f32) / ROPE_PAIRS)
    lane = jnp.arange(LANES)
    j = lane % A_DH
    axis, half, pair = j // (2 * ROPE_PAIRS), (j % (2 * ROPE_PAIRS)) // ROPE_PAIRS, j % ROPE_PAIRS
    ang = jnp.where(axis[None, :] == 0, row[:, None], col[:, None]) * inv[pair][None, :]
    cos = jnp.cos(ang)
    sin = jnp.sin(ang) * jnp.where(half == 0, -1.0, 1.0)[None, :]
    cos = jnp.concatenate([cos, jnp.ones((cfg.tm, LANES), f32)], 0)
    sin = jnp.concatenate([sin, jnp.zeros((cfg.tm, LANES), f32)], 0)
    qs = A_SCALE * LOG2E
    return jnp.stack([cos * qs, cos]), jnp.stack([sin * qs, sin])


_NT = (((1,), (1,)), ((), ()))


def _attn_kernel(tq, nq, n_ctx, n_lat, kchunk, lam_init, lq_ref, g_ref, q_ref, kc_ref, vc_ref, *rest):
    if n_lat:
        kl_ref, vl_ref, o_ref, s_a, s_b, m_a, m_b, vaug = rest
    else:
        o_ref, s_a, m_a, vaug = rest
    nk = n_ctx + n_lat

    vaug[0:n_ctx, 0:A_DV] = vc_ref[...]
    if n_lat:
        vaug[n_ctx:nk, 0:A_DV] = vl_ref[...]
    vaug[:, A_DV:2 * A_DV] = jnp.ones((nk, A_DV), bf16)

    chunks = [(kc_ref, 0, n_ctx, 0)]
    chunks += [(kl_ref, c * kchunk, kchunk, n_ctx + c * kchunk) for c in range(n_lat // kchunk)]

    lq = lq_ref[...]
    lam = (jnp.exp(jnp.sum(lq[0:1] * lq[1:2], axis=1, keepdims=True))
           - jnp.exp(jnp.sum(lq[2:3] * lq[3:4], axis=1, keepdims=True)) + lam_init)
    gain = g_ref[...] * (1.0 - lam_init)

    def tile_rows(j):
        return pl.ds(j * tq if isinstance(j, int) else pl.multiple_of(j * tq, tq), tq)

    def scores(j, s_buf, m_buf):
        q = q_ref[tile_rows(j), :]
        lane = lax.broadcasted_iota(jnp.int32, q.shape, 1)
        zero = jnp.zeros_like(q)
        qq = jnp.concatenate([jnp.where(lane < A_DH, q, zero), jnp.where(lane >= A_DH, q, zero)], axis=0)
        mx = None
        for ref, r0, n, c0 in chunks:
            s = lax.dot_general(qq, ref[r0:r0 + n, :], _NT, preferred_element_type=f32)
            s_buf[:, c0:c0 + n] = s
            for cb in range(n // LANES):
                blk = s[:, cb * LANES:(cb + 1) * LANES]
                mx = blk if mx is None else jnp.maximum(mx, blk)
        m_buf[...] = jnp.broadcast_to(jnp.max(mx, axis=1, keepdims=True), m_buf.shape)

    def values(j, s_buf, m_buf):
        m = m_buf[...]
        pv = None
        for ref, r0, n, c0 in chunks:
            e = jnp.concatenate(
                [jnp.exp2(s_buf[:, c0 + cb * LANES:c0 + (cb + 1) * LANES] - m) for cb in range(n // LANES)],
                axis=1).astype(bf16)
            d = jnp.dot(e, vaug[c0:c0 + n, :], preferred_element_type=f32)
            pv = d if pv is None else pv + d
        o = pv[:, 0:A_DV] / pv[:, A_DV:A_DV + 1]
        out = o[0:tq] - lam * o[tq:2 * tq]
        ms = jnp.mean(out * out, -1, keepdims=True)
        y = out * lax.rsqrt(ms + LN_EPS) * gain
        o_ref[tile_rows(j), :] = y.astype(o_ref.dtype)

    if nq == 1:
        scores(0, s_a, m_a)
        values(0, s_a, m_a)
        return

    scores(0, s_a, m_a)

    pairs = 2 if nq % 4 == 0 else 1

    def body(jj, carry):
        for p in range(pairs):
            j0 = 2 * (pairs * jj + p)
            scores(j0 + 1, s_b, m_b)
            values(j0, s_a, m_a)
            scores(jnp.minimum(j0 + 2, nq - 1), s_a, m_a)
            values(j0 + 1, s_b, m_b)
        return carry

    lax.fori_loop(0, nq // (2 * pairs), body, 0)


def _attn_call(cfg, qk, v, lam_qk_l, subln_g_l, lam_init, latent):
    H = cfg.H
    ctx_blk0 = cfg.nlat // cfg.CTX
    ctx_k = pl.BlockSpec((cfg.CTX, A_DV), lambda b, h: (ctx_blk0 + b, H + h))
    ctx_v = pl.BlockSpec((cfg.CTX, A_DV), lambda b, h: (ctx_blk0 + b, h))
    if latent:
        tq, nq, n_lat, rows = cfg.tql, cfg.S // cfg.tql, cfg.S, cfg.S
        assert nq % 2 == 0
        q_spec = pl.BlockSpec((rows, A_DV), lambda b, h: (b, h))
    else:
        tq, nq, n_lat, rows = cfg.CTX, 1, 0, cfg.CTX
        q_spec = pl.BlockSpec((rows, A_DV), lambda b, h: (ctx_blk0 + b, h))
    in_specs = [
        pl.BlockSpec((4, A_DH), lambda b, h: (0, 0)),
        pl.BlockSpec((1, A_DV), lambda b, h: (0, 0)),
        q_spec, ctx_k, ctx_v,
    ]
    args = [lam_qk_l, subln_g_l.reshape(1, A_DV), qk, qk, v]
    nk = cfg.CTX + n_lat
    score_buf = [pltpu.VMEM((2 * tq, nk), f32)]
    max_buf = [pltpu.VMEM((2 * tq, LANES), f32)]
    if latent:
        in_specs += [
            pl.BlockSpec((cfg.S, A_DV), lambda b, h: (b, H + h)),
            pl.BlockSpec((cfg.S, A_DV), lambda b, h: (b, h)),
        ]
        args += [qk, v]
        score_buf, max_buf = score_buf * 2, max_buf * 2
    return pl.pallas_call(
        functools.partial(_attn_kernel, tq, nq, cfg.CTX, n_lat, cfg.kchunk, lam_init),
        grid=(cfg.B, H),
        in_specs=in_specs,
        out_specs=pl.BlockSpec((rows, A_DV), lambda b, h: (b, h)),
        out_shape=jax.ShapeDtypeStruct((cfg.B * rows, cfg.BW), bf16),
        scratch_shapes=[*score_buf, *max_buf, pltpu.VMEM((nk, 2 * A_DV), bf16)],
        compiler_params=_params("parallel", "parallel"),
        name="diff_attn_lat" if latent else "diff_attn_ctx",
    )(*args)


def _gmlp_kernel(u_ref, v_ref, g_ref, b_ref, ws_ref, bs_ref, o_ref):
    vn = _layer_norm(v_ref[...], g_ref[...], b_ref[...]).astype(bf16)
    nch = vn.shape[0] // B_CHUNK
    gdim = vn.shape[1] // B_GROUPS
    for g in range(B_GROUPS):
        cols = slice(g * gdim, (g + 1) * gdim)
        vg = jnp.concatenate([vn[n * B_CHUNK:(n + 1) * B_CHUNK, cols] for n in range(nch)], axis=1)
        s = jnp.dot(ws_ref[g], vg, preferred_element_type=f32) + bs_ref[:, g:g + 1]
        for n in range(nch):
            rows = slice(n * B_CHUNK, (n + 1) * B_CHUNK)
            o_ref[rows, cols] = (u_ref[rows, cols] * s[:, n * gdim:(n + 1) * gdim]).astype(o_ref.dtype)


def _gmlp_call(cfg, zuv, nrows, ln_g, ln_b, ws_l, bs_t):
    t = cfg.tq
    return pl.pallas_call(
        _gmlp_kernel,
        grid=(nrows // t,),
        in_specs=[
            pl.BlockSpec((t, cfg.BW), lambda i: (i, 0)),
            pl.BlockSpec((t, cfg.BW), lambda i: (i, 1)),
            pl.BlockSpec((1, cfg.BW), lambda i: (0, 0)),
            pl.BlockSpec((1, cfg.BW), lambda i: (0, 0)),
            pl.BlockSpec((B_GROUPS, B_CHUNK, B_CHUNK), lambda i: (0, 0, 0)),
            pl.BlockSpec((B_CHUNK, B_GROUPS), lambda i: (0, 0)),
        ],
        out_specs=pl.BlockSpec((t, cfg.BW), lambda i: (i, 0)),
        out_shape=jax.ShapeDtypeStruct((nrows, cfg.BW), bf16),
        compiler_params=_params("parallel"),
        name="gmlp",
    )(zuv, zuv, ln_g.reshape(1, cfg.BW), ln_b.reshape(1, cfg.BW), ws_l, bs_t)


def _pool_kernel(cfg, t, cur_ref, prev_ref, next_ref, wp_ref, sc_ref, o_ref, ext):
    r0 = pl.program_id(0) * t
    is_lat = r0 < cfg.nlat
    n = jnp.where(is_lat, cfg.S, cfg.CTX)
    pos0 = jnp.where(is_lat, r0 % cfg.S, r0 % cfg.CTX)
    halo = lax.broadcasted_iota(jnp.int32, (C_HALO, 1), 0)
    ext[0:C_HALO, :] = jnp.where(pos0 - C_HALO + halo >= 0, prev_ref[...], 0.0)
    ext[C_HALO:C_HALO + t, :] = cur_ref[...]
    ext[C_HALO + t:2 * C_HALO + t, :] = jnp.where(pos0 + t + halo < n, next_ref[...], 0.0)
    pos = pos0 + lax.broadcasted_iota(jnp.int32, (t, 1), 0)
    gdim = cfg.BW // len(C_WINDOWS)
    for g, w in enumerate(C_WINDOWS):
        cols = slice(g * gdim, (g + 1) * gdim)
        acc = ext[C_HALO - w // 2:C_HALO - w // 2 + t, cols]
        for o in range(-w // 2 + 1, w // 2):
            acc = acc + ext[C_HALO + o:C_HALO + o + t, cols]
        lo = jnp.maximum(pos - w // 2, 0)
        hi = jnp.minimum(pos - w // 2 + w, n)
        inv = 1.0 / (hi - lo).astype(f32)
        d = (acc * inv - cur_ref[:, cols]).astype(bf16)
        y = jnp.dot(d, wp_ref[g], preferred_element_type=f32) * sc_ref[:, cols]
        o_ref[:, cols] = y.astype(o_ref.dtype)


def _pool_call(cfg, zc, nrows, wp_l, scale_l):
    t = cfg.tq
    hb = t // C_HALO
    last_hb = zc.shape[0] // C_HALO - 1
    gdim = cfg.BW // len(C_WINDOWS)
    return pl.pallas_call(
        functools.partial(_pool_kernel, cfg, t),
        grid=(nrows // t,),
        in_specs=[
            pl.BlockSpec((t, cfg.BW), lambda i: (i, 0)),
            pl.BlockSpec((C_HALO, cfg.BW), lambda i: (jnp.maximum(i * hb - 1, 0), 0)),
            pl.BlockSpec((C_HALO, cfg.BW), lambda i: (jnp.minimum((i + 1) * hb, last_hb), 0)),
            pl.BlockSpec((len(C_WINDOWS), gdim, gdim), lambda i: (0, 0, 0)),
            pl.BlockSpec((1, cfg.BW), lambda i: (0, 0)),
        ],
        out_specs=pl.BlockSpec((t, cfg.BW), lambda i: (i, 0)),
        out_shape=jax.ShapeDtypeStruct((nrows, cfg.BW), bf16),
        scratch_shapes=[pltpu.VMEM((t + 2 * C_HALO, cfg.BW), f32)],
        compiler_params=_params("parallel"),
        name="pool",
    )(zc, zc, zc, wp_l, scale_l.reshape(1, cfg.BW))


def _merge_kernel(alpha, nlat_tiles, has_ctx, ya_ref, *refs):
    ya = ya_ref[...]
    if has_ctx:
        yac_ref, refs = refs[0], refs[1:]
        tile = jnp.full(ya.shape, pl.program_id(0), jnp.int32)
        ya = jnp.where(tile < nlat_tiles, ya, yac_ref[...])
    yb_ref, yc_ref, g_ref, wb_ref, wo_ref, x_ref, gm_ref, shf_ref, scf_ref, lg_ref, lb_ref, xo_ref, ho_ref = refs
    d = x_ref.shape[1]
    acc = None
    for n, y in enumerate((ya, yb_ref[...], yc_ref[...])):
        p = g_ref[:, n * d:(n + 1) * d] * jnp.dot(y, wb_ref[n], preferred_element_type=f32)
        acc = p if acc is None else acc + p
    out = jnp.dot(acc.astype(bf16), wo_ref[...], preferred_element_type=f32)
    xn = _layer_norm(alpha * x_ref[...] + gm_ref[...] * out, lg_ref[...], lb_ref[...])
    xo_ref[...] = xn
    ho_ref[...] = (xn * (1.0 + scf_ref[...]) + shf_ref[...]).astype(ho_ref.dtype)


def _merge_call(cfg, ya, ya_ctx, yb, yc, gates, xs, nrows, wb, w_out, mods, layer, alpha, ln_g, ln_b):
    t = cfg.ts
    nlat_tiles = cfg.nlat // t
    row = lambda i: (i, 0)
    vec = pl.BlockSpec((1, cfg.D), lambda i: (0, 0))
    y_spec = pl.BlockSpec((t, cfg.BW), row)
    ya_specs = [pl.BlockSpec((t, cfg.BW), lambda i: (jnp.minimum(i, nlat_tiles - 1), 0))]
    ya_args = [ya]
    if ya_ctx is not None:
        ya_specs.append(pl.BlockSpec((t, cfg.BW), lambda i: (jnp.maximum(i - nlat_tiles, 0), 0)))
        ya_args.append(ya_ctx)
    return pl.pallas_call(
        functools.partial(_merge_kernel, alpha, nlat_tiles, ya_ctx is not None),
        grid=(nrows // t,),
        in_specs=[
            *ya_specs, y_spec, y_spec,
            pl.BlockSpec((t, 3 * cfg.D), row),
            pl.BlockSpec((None, 3, cfg.BW, cfg.D), lambda i: (layer, 0, 0, 0), pipeline_mode=pl.Buffered(1)),
            pl.BlockSpec((None, cfg.D, cfg.D), lambda i: (layer, 0, 0), pipeline_mode=pl.Buffered(1)),
            pl.BlockSpec((t, cfg.D), row),
            _mod_spec(cfg, layer, 2, t), _mod_spec(cfg, layer, 3, t), _mod_spec(cfg, layer, 4, t),
            vec, vec,
        ],
        out_specs=[pl.BlockSpec((t, cfg.D), row), pl.BlockSpec((t, cfg.D), row)],
        out_shape=[jax.ShapeDtypeStruct((nrows, cfg.D), f32), jax.ShapeDtypeStruct((nrows, cfg.D), bf16)],
        compiler_params=_params("parallel"),
        name="merge_outproj_ln",
    )(*ya_args, yb, yc, gates, wb, w_out, xs, mods, mods, mods, ln_g.reshape(1, cfg.D), ln_b.reshape(1, cfg.D))


def _ffn_up_kernel(h_ref, wg_ref, wu_ref, o_ref, wgb, wub):
    @pl.when(pl.program_id(1) == 0)
    def _():
        wgb[...] = wg_ref[...].astype(bf16)
        wub[...] = wu_ref[...].astype(bf16)

    h = h_ref[...]
    g = jnp.dot(h, wgb[...], preferred_element_type=f32)
    u = jnp.dot(h, wub[...], preferred_element_type=f32)
    o_ref[...] = (g * jax.nn.sigmoid(g) * u).astype(o_ref.dtype)


def _ffn_up_call(cfg, hf, nrows, w_gu, layer):
    tm, tf = cfg.tm, cfg.tf
    nj = cfg.FH // tf
    return pl.pallas_call(
        _ffn_up_kernel,
        grid=(nj, nrows // tm),
        in_specs=[
            pl.BlockSpec((tm, cfg.D), lambda j, i: (i, 0)),
            pl.BlockSpec((None, cfg.D, tf), lambda j, i: (layer, 0, j)),
            pl.BlockSpec((None, cfg.D, tf), lambda j, i: (layer, 0, nj + j)),
        ],
        out_specs=pl.BlockSpec((tm, tf), lambda j, i: (i, j)),
        out_shape=jax.ShapeDtypeStruct((nrows, cfg.FH), bf16),
        scratch_shapes=[pltpu.VMEM((cfg.D, tf), bf16), pltpu.VMEM((cfg.D, tf), bf16)],
        compiler_params=_params("parallel", "arbitrary"),
        name="ffn_up",
    )(hf, w_gu, w_gu)


def _ffn_down_kernel(alpha, has_next, act_ref, w_ref, x_ref, gf_ref, lg_ref, lb_ref, *rest):
    if has_next:
        shn_ref, scn_ref, xo_ref, ho_ref = rest
    else:
        (xo_ref,) = rest
    y = jnp.dot(act_ref[...], w_ref[...], preferred_element_type=f32)
    xn = _layer_norm(alpha * x_ref[...] + gf_ref[...] * y, lg_ref[...], lb_ref[...])
    xo_ref[...] = xn
    if has_next:
        ho_ref[...] = (xn * (1.0 + scn_ref[...]) + shn_ref[...]).astype(ho_ref.dtype)


def _ffn_down_call(cfg, act, xs, nrows, w_down, mods, layer, alpha, ln_g, ln_b):
    t = cfg.ts
    has_next = layer + 1 < cfg.DEPTH
    row = lambda i: (i, 0)
    vec = pl.BlockSpec((1, cfg.D), lambda i: (0, 0))
    in_specs = [
        pl.BlockSpec((t, cfg.FH), row),
        pl.BlockSpec((None, cfg.FH, cfg.D), lambda i: (layer, 0, 0), pipeline_mode=pl.Buffered(1)),
        pl.BlockSpec((t, cfg.D), row),
        _mod_spec(cfg, layer, 5, t), vec, vec,
    ]
    args = [act, w_down, xs, mods, ln_g.reshape(1, cfg.D), ln_b.reshape(1, cfg.D)]
    out_specs = [pl.BlockSpec((t, cfg.D), row)]
    out_shape = [jax.ShapeDtypeStruct((nrows, cfg.D), f32)]
    if has_next:
        in_specs += [_mod_spec(cfg, layer + 1, 0, t), _mod_spec(cfg, layer + 1, 1, t)]
        args += [mods, mods]
        out_specs.append(pl.BlockSpec((t, cfg.D), row))
        out_shape.append(jax.ShapeDtypeStruct((nrows, cfg.D), bf16))
    return pl.pallas_call(
        functools.partial(_ffn_down_kernel, alpha, has_next),
        grid=(nrows // t,),
        in_specs=in_specs,
        out_specs=out_specs,
        out_shape=out_shape,
        compiler_params=_params("parallel"),
        name="ffn_down_ln",
    )(*args)


def kernel(x, c, ctx, c_ctx, w_ada, b_ada, w_in, lam_qk, subln_g, gmlp_ln_g, gmlp_ln_b,
           w_spatial, b_spatial, w_pool, pool_scale, w_branch, w_out, ln1_g, ln1_b,
           w_gu, w_down, ln2_g, ln2_b):
    cfg = _make_cfg(x, ctx, w_ada, w_gu)
    D, BW = cfg.D, cfg.BW
    alpha = (2 * cfg.DEPTH) ** 0.25

    cond = jnp.zeros((MOD_ROWS, D), f32).at[:cfg.B].set(c).at[cfg.B].set(c_ctx)
    mods = _ada_call(cfg, cond, w_ada, b_ada).reshape(cfg.DEPTH * MOD_ROWS * N_MOD, 1, D)
    tables = _rope_tables(cfg)

    w_sp_b, w_pool_b = w_spatial.astype(bf16), w_pool.astype(bf16)
    w_br_b, w_out_b, w_down_b = w_branch.astype(bf16), w_out.astype(bf16), w_down.astype(bf16)

    xs, h = _entry_call(cfg, x.reshape(cfg.nlat, D), ctx.reshape(cfg.nctx, D), mods)

    for l in range(cfg.DEPTH):
        last = l == cfg.DEPTH - 1
        nrows = cfg.nlat if last else cfg.ntok
        lam_init = 0.8 - 0.6 * math.exp(-0.3 * l)

        qk = _inproj_call(cfg, h, w_in, l, 0, 2 * BW, cfg.ntok, "rope", bf16, tables)
        v = _inproj_call(cfg, h, w_in, l, 2 * BW, BW, cfg.ntok, "cast", bf16)
        zuv = _inproj_call(cfg, h, w_in, l, 3 * BW, 2 * BW, nrows, "gelu", f32)
        zc = _inproj_call(cfg, h, w_in, l, 5 * BW, BW, nrows, "copy", f32)
        gates = _inproj_call(cfg, h, w_in, l, 6 * BW, 3 * D, nrows, "sigmoid", f32)

        ya = _attn_call(cfg, qk, v, lam_qk[l], subln_g[l], lam_init, latent=True)
        ya_ctx = None if last else _attn_call(cfg, qk, v, lam_qk[l], subln_g[l], lam_init, latent=False)
        yb = _gmlp_call(cfg, zuv, nrows, gmlp_ln_g[l], gmlp_ln_b[l], w_sp_b[l], b_spatial[l].T)
        yc = _pool_call(cfg, zc, nrows, w_pool_b[l], pool_scale[l])

        xs, hf = _merge_call(cfg, ya, ya_ctx, yb, yc, gates, xs, nrows, w_br_b, w_out_b, mods, l, alpha,
                             ln1_g[l], ln1_b[l])
        act = _ffn_up_call(cfg, hf, nrows, w_gu, l)
        res = _ffn_down_call(cfg, act, xs, nrows, w_down_b, mods, l, alpha, ln2_g[l], ln2_b[l])
        if last:
            xs = res[0]
        else:
            xs, h = res

    return xs.reshape(cfg.B, cfg.S, D)
```

```python
import functools
import math
from typing import NamedTuple

import jax
import jax.numpy as jnp
from jax import lax
from jax.experimental import pallas as pl
from jax.experimental.pallas import tpu as pltpu

f32 = jnp.float32
bf16 = jnp.bfloat16

GRID_W = 64
A_DH = 64
A_DV = 2 * A_DH
ROPE_THETA = 10000.0
ROPE_PAIRS = A_DH // 4
A_SCALE = A_DH ** -0.5
B_CHUNK = 128
B_GROUPS = 8
C_WINDOWS = (2, 4, 8, 16)
C_HALO = 8
LN_EPS = 1e-6
N_MOD = 6
MOD_ROWS = 8
LOG2E = math.log2(math.e)

LANES = 128
SUBLANES = 8
VMEM_LIMIT_BYTES = 56 * 1024 * 1024


class Cfg(NamedTuple):
    B: int
    S: int
    CTX: int
    D: int
    DEPTH: int
    BW: int
    H: int
    FH: int
    nlat: int
    nctx: int
    ntok: int
    tm: int
    tr: int
    ts: int
    tq: int
    tql: int
    kchunk: int
    tf: int


def _pick(cands, *divs):
    for t in cands:
        if all(d % t == 0 for d in divs):
            return t
    raise ValueError(f"no tile in {cands} divides {divs}")


def _make_cfg(x, ctx, w_ada, w_gu):
    B, S, D = x.shape
    CTX = ctx.shape[1]
    BW = D // 2
    FH = w_gu.shape[2] // 2
    nlat, nctx = B * S, B * ctx.shape[1]
    assert B + 1 <= MOD_ROWS and S % GRID_W == 0 and nlat % CTX == 0
    assert CTX % B_CHUNK == 0 and S % B_CHUNK == 0 and BW % (4 * LANES) == 0
    tm = _pick((1024, 512, 256, 128), S, nctx)
    tr = _pick((512, 256, 128), S, nctx)
    ts = _pick((256, 128), S, nctx)
    tq = _pick((256, 128), S, CTX)
    tql = _pick((128,), S // 2)
    kchunk = _pick((512, 256, 128), S)
    tf = _pick((512, 256, 128), FH)
    return Cfg(B, S, CTX, D, w_ada.shape[0], BW, BW // A_DV, FH, nlat, nctx, nlat + nctx,
               tm, tr, ts, tq, tql, kchunk, tf)


def _params(*sem):
    return pltpu.CompilerParams(dimension_semantics=sem, vmem_limit_bytes=VMEM_LIMIT_BYTES)


def _mod_spec(cfg, layer, chunk, tile):
    nlat_tiles = cfg.nlat // tile

    def idx(i, *_):
        r = jnp.where(i < nlat_tiles, (i * tile) // cfg.S, cfg.B)
        return ((layer * MOD_ROWS + r) * N_MOD + chunk, 0, 0)

    return pl.BlockSpec((None, 1, cfg.D), idx)


def _layer_norm(xf, g, b):
    mu = jnp.mean(xf, -1, keepdims=True)
    xc = xf - mu
    var = jnp.mean(xc * xc, -1, keepdims=True)
    return xc * lax.rsqrt(var + LN_EPS) * g + b


def _ada_kernel(c_ref, w_ref, b_ref, o_ref):
    cv = c_ref[...]
    a = (cv * jax.nn.sigmoid(cv)).astype(bf16)
    o_ref[...] = jnp.dot(a, w_ref[...].astype(bf16), preferred_element_type=f32) + b_ref[...]


def _ada_call(cfg, cond, w_ada, b_ada):
    width = N_MOD * cfg.D
    tn = _pick((1024, 512, 256, 128), width)
    return pl.pallas_call(
        _ada_kernel,
        grid=(cfg.DEPTH, width // tn),
        in_specs=[
            pl.BlockSpec((MOD_ROWS, cfg.D), lambda l, j: (0, 0)),
            pl.BlockSpec((None, cfg.D, tn), lambda l, j: (l, 0, j)),
            pl.BlockSpec((None, 1, tn), lambda l, j: (l, 0, j)),
        ],
        out_specs=pl.BlockSpec((None, MOD_ROWS, tn), lambda l, j: (l, 0, j)),
        out_shape=jax.ShapeDtypeStruct((cfg.DEPTH, MOD_ROWS, width), f32),
        compiler_params=_params("parallel", "parallel"),
        name="adaln_table",
    )(cond, w_ada, b_ada.reshape(cfg.DEPTH, 1, width))


def _entry_kernel(nlat_tiles, x_ref, c_ref, sh_ref, sc_ref, xo_ref, ho_ref):
    def body(src_ref):
        xf = src_ref[...]
        mu = jnp.mean(xf, -1, keepdims=True)
        xc = xf - mu
        var = jnp.mean(xc * xc, -1, keepdims=True)
        xn = xc * lax.rsqrt(var + LN_EPS)
        xo_ref[...] = xn
        ho_ref[...] = (xn * (1.0 + sc_ref[...]) + sh_ref[...]).astype(bf16)

    i = pl.program_id(0)
    pl.when(i < nlat_tiles)(lambda: body(x_ref))
    pl.when(i >= nlat_tiles)(lambda: body(c_ref))


def _entry_call(cfg, x2, c2, mods):
    t = cfg.tr
    nlat_tiles = cfg.nlat // t
    row = lambda i: (i, 0)
    return pl.pallas_call(
        functools.partial(_entry_kernel, nlat_tiles),
        grid=(cfg.ntok // t,),
        in_specs=[
            pl.BlockSpec((t, cfg.D), lambda i: (jnp.minimum(i, nlat_tiles - 1), 0)),
            pl.BlockSpec((t, cfg.D), lambda i: (jnp.maximum(i - nlat_tiles, 0), 0)),
            _mod_spec(cfg, 0, 0, t),
            _mod_spec(cfg, 0, 1, t),
        ],
        out_specs=[pl.BlockSpec((t, cfg.D), row), pl.BlockSpec((t, cfg.D), row)],
        out_shape=[jax.ShapeDtypeStruct((cfg.ntok, cfg.D), f32),
                   jax.ShapeDtypeStruct((cfg.ntok, cfg.D), bf16)],
        compiler_params=_params("parallel"),
        name="entry_norm",
    )(x2, c2, mods, mods)


def _inproj_kernel(kind, h_ref, w_ref, *refs):
    o_ref, wb = refs[-2], refs[-1]

    @pl.when(pl.program_id(1) == 0)
    def _():
        wb[...] = w_ref[...].astype(bf16)

    acc = jnp.dot(h_ref[...], wb[...], preferred_element_type=f32)
    if kind == "rope":
        cos_ref, sin_ref = refs[0], refs[1]
        cs, sn = cos_ref[...], sin_ref[...]
        lane = lax.broadcasted_iota(jnp.int32, cs.shape, 1)
        first_half = (lane % (2 * ROPE_PAIRS)) < ROPE_PAIRS
        for cb in range(acc.shape[1] // LANES):
            cols = slice(cb * LANES, (cb + 1) * LANES)
            z = acc[:, cols]
            sw = jnp.where(first_half, pltpu.roll(z, LANES - ROPE_PAIRS, 1), pltpu.roll(z, ROPE_PAIRS, 1))
            o_ref[:, cols] = (z * cs + sw * sn).astype(o_ref.dtype)
    elif kind == "gelu":
        o_ref[...] = (0.5 * acc * (1.0 + lax.erf(acc * (2.0 ** -0.5)))).astype(o_ref.dtype)
    elif kind == "sigmoid":
        o_ref[...] = jax.nn.sigmoid(acc).astype(o_ref.dtype)
    else:
        o_ref[...] = acc.astype(o_ref.dtype)


def _inproj_call(cfg, h, w, layer, col0, ncols, nrows, kind, out_dtype, tables=None):
    tm = cfg.tm
    tn = _pick((1024, 512, 256, 128), ncols, col0)
    j0 = col0 // tn
    in_specs = [
        pl.BlockSpec((tm, cfg.D), lambda j, i: (i, 0)),
        pl.BlockSpec((None, cfg.D, tn), lambda j, i: (layer, 0, j0 + j)),
    ]
    args = [h, w]
    if kind == "rope":
        nlat_tiles, per_seq = cfg.nlat // tm, cfg.S // tm

        def tab_idx(j, i):
            return (j * tn // cfg.BW, jnp.where(i < nlat_tiles, i % per_seq, per_seq), 0)

        in_specs += [pl.BlockSpec((None, tm, LANES), tab_idx)] * 2
        args += list(tables)
    return pl.pallas_call(
        functools.partial(_inproj_kernel, kind),
        grid=(ncols // tn, nrows // tm),
        in_specs=in_specs,
        out_specs=pl.BlockSpec((tm, tn), lambda j, i: (i, j)),
        out_shape=jax.ShapeDtypeStruct((nrows, ncols), out_dtype),
        scratch_shapes=[pltpu.VMEM((cfg.D, tn), bf16)],
        compiler_params=_params("parallel", "arbitrary"),
        name="inproj_" + kind,
    )(*args)


def _rope_tables(cfg):
    t = jnp.arange(cfg.S)
    row, col = (t // GRID_W).astype(f32), (t % GRID_W).astype(f32)
    inv = ROPE_THETA ** (-jnp.arange(ROPE_PAIRS, dtype=f32) / ROPE_PAIRS)
    lane = jnp.arange(LANES)
    j = lane % A_DH
    axis, half, pair = j // (2 * ROPE_PAIRS), (j % (2 * ROPE_PAIRS)) // ROPE_PAIRS, j % ROPE_PAIRS
    ang = jnp.where(axis[None, :] == 0, row[:, None], col[:, None]) * inv[pair][None, :]
    cos = jnp.cos(ang)
    sin = jnp.sin(ang) * jnp.where(half == 0, -1.0, 1.0)[None, :]
    cos = jnp.concatenate([cos, jnp.ones((cfg.tm, LANES), f32)], 0)
    sin = jnp.concatenate([sin, jnp.zeros((cfg.tm, LANES), f32)], 0)
    qs = A_SCALE * LOG2E
    return jnp.stack([cos * qs, cos]), jnp.stack([sin * qs, sin])


_NT = (((1,), (1,)), ((), ()))


def _attn_kernel(tq, nq, n_ctx, n_lat, kchunk, lam_init, lq_ref, g_ref, q_ref, kc_ref, vc_ref, *rest):
    if n_lat:
        kl_ref, vl_ref, o_ref, s_a, s_b, m_a, m_b, vaug = rest
    else:
        o_ref, s_a, m_a, vaug = rest
    nk = n_ctx + n_lat

    vaug[0:n_ctx, 0:A_DV] = vc_ref[...]
    if n_lat:
        vaug[n_ctx:nk, 0:A_DV] = vl_ref[...]
    vaug[:, A_DV:2 * A_DV] = jnp.ones((nk, A_DV), bf16)

    chunks = [(kc_ref, 0, n_ctx, 0)]
    chunks += [(kl_ref, c * kchunk, kchunk, n_ctx + c * kchunk) for c in range(n_lat // kchunk)]

    lq = lq_ref[...]
    lam = (jnp.exp(jnp.sum(lq[0:1] * lq[1:2], axis=1, keepdims=True))
           - jnp.exp(jnp.sum(lq[2:3] * lq[3:4], axis=1, keepdims=True)) + lam_init)
    gain = g_ref[...] * (1.0 - lam_init)

    def tile_rows(j):
        return pl.ds(j * tq if isinstance(j, int) else pl.multiple_of(j * tq, tq), tq)

    def scores(j, s_buf, m_buf):
        q = q_ref[tile_rows(j), :]
        lane = lax.broadcasted_iota(jnp.int32, q.shape, 1)
        zero = jnp.zeros_like(q)
        qq = jnp.concatenate([jnp.where(lane < A_DH, q, zero), jnp.where(lane >= A_DH, q, zero)], axis=0)
        mx = None
        for ref, r0, n, c0 in chunks:
            s = lax.dot_general(qq, ref[r0:r0 + n, :], _NT, preferred_element_type=f32)
            s_buf[:, c0:c0 + n] = s
            for cb in range(n // LANES):
                blk = s[:, cb * LANES:(cb + 1) * LANES]
                mx = blk if mx is None else jnp.maximum(mx, blk)
        m_buf[...] = jnp.broadcast_to(jnp.max(mx, axis=1, keepdims=True), m_buf.shape)

    def values(j, s_buf, m_buf):
        m = m_buf[...]
        pv = None
        for ref, r0, n, c0 in chunks:
            e = jnp.concatenate(
                [jnp.exp2(s_buf[:, c0 + cb * LANES:c0 + (cb + 1) * LANES] - m) for cb in range(n // LANES)],
                axis=1).astype(bf16)
            d = jnp.dot(e, vaug[c0:c0 + n, :], preferred_element_type=f32)
            pv = d if pv is None else pv + d
        o = pv[:, 0:A_DV] / pv[:, A_DV:A_DV + 1]
        out = o[0:tq] - lam * o[tq:2 * tq]
        ms = jnp.mean(out * out, -1, keepdims=True)
        y = out * lax.rsqrt(ms + LN_EPS) * gain
        o_ref[tile_rows(j), :] = y.astype(o_ref.dtype)

    if nq == 1:
        scores(0, s_a, m_a)
        values(0, s_a, m_a)
        return

    scores(0, s_a, m_a)

    pairs = 2 if nq % 4 == 0 else 1

    def body(jj, carry):
        for p in range(pairs):
            j0 = 2 * (pairs * jj + p)
            scores(j0 + 1, s_b, m_b)
            values(j0, s_a, m_a)
            scores(jnp.minimum(j0 + 2, nq - 1), s_a, m_a)
            values(j0 + 1, s_b, m_b)
        return carry

    lax.fori_loop(0, nq // (2 * pairs), body, 0)


def _attn_t_kernel(tq, nq, n_ctx, n_lat, kchunk, lam_init, lq_ref, gt_ref, q_ref, kc_ref, vc_ref,
                   kl_ref, vl_ref, o_ref, s_a, s_b, m_a, m_b, vt):
    nk = n_ctx + n_lat
    vt[:, 0:n_ctx] = vc_ref[...].T
    for c in range(n_lat // kchunk):
        vt[:, n_ctx + c * kchunk:n_ctx + (c + 1) * kchunk] = vl_ref[c * kchunk:(c + 1) * kchunk, :].T

    chunks = [(kc_ref, 0, n_ctx, 0)]
    chunks += [(kl_ref, c * kchunk, kchunk, n_ctx + c * kchunk) for c in range(n_lat // kchunk)]

    lq = lq_ref[...]
    lam = (jnp.exp(jnp.sum(lq[0:1] * lq[1:2], axis=1, keepdims=True))
           - jnp.exp(jnp.sum(lq[2:3] * lq[3:4], axis=1, keepdims=True)) + lam_init)
    gain = gt_ref[...] * (1.0 - lam_init)

    def tile_rows(j):
        return pl.ds(pl.multiple_of(j * tq, tq), tq)

    def scores(j, s_buf, m_buf):
        q = q_ref[tile_rows(j), :]
        lane = lax.broadcasted_iota(jnp.int32, q.shape, 1)
        zero = jnp.zeros_like(q)
        qqt = jnp.concatenate([jnp.where(lane < A_DH, q, zero), jnp.where(lane >= A_DH, q, zero)], axis=0).T
        mx = None
        for ref, r0, n, c0 in chunks:
            st = jnp.dot(ref[r0:r0 + n, :], qqt, preferred_element_type=f32)
            s_buf[c0:c0 + n, :] = st
            cm = jnp.max(st, axis=0, keepdims=True)
            mx = cm if mx is None else jnp.maximum(mx, cm)
        m_buf[...] = jnp.broadcast_to(mx, m_buf.shape)

    def values(j, s_buf, m_buf):
        m = m_buf[0:1, :]
        acc, l = None, None
        for ref, r0, n, c0 in chunks:
            e = jnp.exp2(s_buf[c0:c0 + n, :] - m)
            ls = jnp.sum(e, axis=0, keepdims=True)
            d = jnp.dot(vt[:, c0:c0 + n], e.astype(bf16), preferred_element_type=f32)
            acc, l = (d, ls) if acc is None else (acc + d, l + ls)
        o = acc / l
        out = o[:, 0:tq] - lam * o[:, tq:2 * tq]
        ms = jnp.mean(out * out, axis=0, keepdims=True)
        y = out * lax.rsqrt(ms + LN_EPS) * gain
        o_ref[tile_rows(j), :] = y.T.astype(o_ref.dtype)

    scores(0, s_a, m_a)
    pairs = max(p for p in (4, 2, 1) if nq % (2 * p) == 0)

    def body(jj, carry):
        for p in range(pairs):
            j0 = 2 * (pairs * jj + p)
            scores(j0 + 1, s_b, m_b)
            values(j0, s_a, m_a)
            scores(jnp.minimum(j0 + 2, nq - 1), s_a, m_a)
            values(j0 + 1, s_b, m_b)
        return carry

    lax.fori_loop(0, nq // (2 * pairs), body, 0)


def _attn_call(cfg, qk, v, lam_qk_l, subln_g_l, lam_init, latent):
    H = cfg.H
    ctx_blk0 = cfg.nlat // cfg.CTX
    ctx_k = pl.BlockSpec((cfg.CTX, A_DV), lambda b, h: (ctx_blk0 + b, H + h))
    ctx_v = pl.BlockSpec((cfg.CTX, A_DV), lambda b, h: (ctx_blk0 + b, h))
    if latent:
        tq, nq, n_lat, rows = cfg.tql, cfg.S // cfg.tql, cfg.S, cfg.S
        assert nq % 2 == 0
        q_spec = pl.BlockSpec((rows, A_DV), lambda b, h: (b, h))
    else:
        tq, nq, n_lat, rows = cfg.CTX, 1, 0, cfg.CTX
        q_spec = pl.BlockSpec((rows, A_DV), lambda b, h: (ctx_blk0 + b, h))
    nk = cfg.CTX + n_lat
    if latent:
        body = _attn_t_kernel
        gain_shape = (A_DV, 1)
        extra_specs = [
            pl.BlockSpec((cfg.S, A_DV), lambda b, h: (b, H + h)),
            pl.BlockSpec((cfg.S, A_DV), lambda b, h: (b, h)),
        ]
        extra_args = [qk, v]
        scratch = [pltpu.VMEM((nk, 2 * tq), f32)] * 2 + [pltpu.VMEM((SUBLANES, 2 * tq), f32)] * 2
        scratch.append(pltpu.VMEM((A_DV, nk), bf16))
    else:
        body = _attn_kernel
        gain_shape = (1, A_DV)
        extra_specs, extra_args = [], []
        scratch = [pltpu.VMEM((2 * tq, nk), f32), pltpu.VMEM((2 * tq, LANES), f32),
                   pltpu.VMEM((nk, 2 * A_DV), bf16)]
    in_specs = [
        pl.BlockSpec((4, A_DH), lambda b, h: (0, 0)),
        pl.BlockSpec(gain_shape, lambda b, h: (0, 0)),
        q_spec, ctx_k, ctx_v, *extra_specs,
    ]
    args = [lam_qk_l, subln_g_l.reshape(gain_shape), qk, qk, v, *extra_args]
    return pl.pallas_call(
        functools.partial(body, tq, nq, cfg.CTX, n_lat, cfg.kchunk, lam_init),
        grid=(cfg.B, H),
        in_specs=in_specs,
        out_specs=pl.BlockSpec((rows, A_DV), lambda b, h: (b, h)),
        out_shape=jax.ShapeDtypeStruct((cfg.B * rows, cfg.BW), bf16),
        scratch_shapes=scratch,
        compiler_params=_params("parallel", "parallel"),
        name="diff_attn_lat" if latent else "diff_attn_ctx",
    )(*args)


def _gmlp_kernel(u_ref, v_ref, g_ref, b_ref, ws_ref, bs_ref, o_ref):
    vn = _layer_norm(v_ref[...], g_ref[...], b_ref[...]).astype(bf16)
    nch = vn.shape[0] // B_CHUNK
    gdim = vn.shape[1] // B_GROUPS
    for g in range(B_GROUPS):
        cols = slice(g * gdim, (g + 1) * gdim)
        vg = jnp.concatenate([vn[n * B_CHUNK:(n + 1) * B_CHUNK, cols] for n in range(nch)], axis=1)
        s = jnp.dot(ws_ref[g], vg, preferred_element_type=f32) + bs_ref[:, g:g + 1]
        for n in range(nch):
            rows = slice(n * B_CHUNK, (n + 1) * B_CHUNK)
            o_ref[rows, cols] = (u_ref[rows, cols] * s[:, n * gdim:(n + 1) * gdim]).astype(o_ref.dtype)


def _gmlp_call(cfg, zuv, nrows, ln_g, ln_b, ws_l, bs_t):
    t = cfg.tq
    return pl.pallas_call(
        _gmlp_kernel,
        grid=(nrows // t,),
        in_specs=[
            pl.BlockSpec((t, cfg.BW), lambda i: (i, 0)),
            pl.BlockSpec((t, cfg.BW), lambda i: (i, 1)),
            pl.BlockSpec((1, cfg.BW), lambda i: (0, 0)),
            pl.BlockSpec((1, cfg.BW), lambda i: (0, 0)),
            pl.BlockSpec((B_GROUPS, B_CHUNK, B_CHUNK), lambda i: (0, 0, 0)),
            pl.BlockSpec((B_CHUNK, B_GROUPS), lambda i: (0, 0)),
        ],
        out_specs=pl.BlockSpec((t, cfg.BW), lambda i: (i, 0)),
        out_shape=jax.ShapeDtypeStruct((nrows, cfg.BW), bf16),
        compiler_params=_params("parallel"),
        name="gmlp",
    )(zuv, zuv, ln_g.reshape(1, cfg.BW), ln_b.reshape(1, cfg.BW), ws_l, bs_t)


def _pool_kernel(cfg, t, cur_ref, prev_ref, next_ref, wp_ref, sc_ref, o_ref, ext):
    r0 = pl.program_id(0) * t
    is_lat = r0 < cfg.nlat
    n = jnp.where(is_lat, cfg.S, cfg.CTX)
    pos0 = jnp.where(is_lat, r0 % cfg.S, r0 % cfg.CTX)
    halo = lax.broadcasted_iota(jnp.int32, (C_HALO, 1), 0)
    ext[0:C_HALO, :] = jnp.where(pos0 - C_HALO + halo >= 0, prev_ref[...], 0.0)
    ext[C_HALO:C_HALO + t, :] = cur_ref[...]
    ext[C_HALO + t:2 * C_HALO + t, :] = jnp.where(pos0 + t + halo < n, next_ref[...], 0.0)
    pos = pos0 + lax.broadcasted_iota(jnp.int32, (t, 1), 0)
    gdim = cfg.BW // len(C_WINDOWS)
    for g, w in enumerate(C_WINDOWS):
        cols = slice(g * gdim, (g + 1) * gdim)
        acc = ext[C_HALO - w // 2:C_HALO - w // 2 + t, cols]
        for o in range(-w // 2 + 1, w // 2):
            acc = acc + ext[C_HALO + o:C_HALO + o + t, cols]
        lo = jnp.maximum(pos - w // 2, 0)
        hi = jnp.minimum(pos - w // 2 + w, n)
        inv = 1.0 / (hi - lo).astype(f32)
        d = (acc * inv - cur_ref[:, cols]).astype(bf16)
        y = jnp.dot(d, wp_ref[g], preferred_element_type=f32) * sc_ref[:, cols]
        o_ref[:, cols] = y.astype(o_ref.dtype)


def _pool_call(cfg, zc, nrows, wp_l, scale_l):
    t = cfg.tq
    hb = t // C_HALO
    last_hb = zc.shape[0] // C_HALO - 1
    gdim = cfg.BW // len(C_WINDOWS)
    return pl.pallas_call(
        functools.partial(_pool_kernel, cfg, t),
        grid=(nrows // t,),
        in_specs=[
            pl.BlockSpec((t, cfg.BW), lambda i: (i, 0)),
            pl.BlockSpec((C_HALO, cfg.BW), lambda i: (jnp.maximum(i * hb - 1, 0), 0)),
            pl.BlockSpec((C_HALO, cfg.BW), lambda i: (jnp.minimum((i + 1) * hb, last_hb), 0)),
            pl.BlockSpec((len(C_WINDOWS), gdim, gdim), lambda i: (0, 0, 0)),
            pl.BlockSpec((1, cfg.BW), lambda i: (0, 0)),
        ],
        out_specs=pl.BlockSpec((t, cfg.BW), lambda i: (i, 0)),
        out_shape=jax.ShapeDtypeStruct((nrows, cfg.BW), bf16),
        scratch_shapes=[pltpu.VMEM((t + 2 * C_HALO, cfg.BW), f32)],
        compiler_params=_params("parallel"),
        name="pool",
    )(zc, zc, zc, wp_l, scale_l.reshape(1, cfg.BW))


def _merge_kernel(alpha, nlat_tiles, has_ctx, ya_ref, *refs):
    ya = ya_ref[...]
    if has_ctx:
        yac_ref, refs = refs[0], refs[1:]
        tile = jnp.full(ya.shape, pl.program_id(0), jnp.int32)
        ya = jnp.where(tile < nlat_tiles, ya, yac_ref[...])
    yb_ref, yc_ref, g_ref, wb_ref, wo_ref, x_ref, gm_ref, shf_ref, scf_ref, lg_ref, lb_ref, xo_ref, ho_ref = refs
    d = x_ref.shape[1]
    acc = None
    for n, y in enumerate((ya, yb_ref[...], yc_ref[...])):
        p = g_ref[:, n * d:(n + 1) * d] * jnp.dot(y, wb_ref[n], preferred_element_type=f32)
        acc = p if acc is None else acc + p
    out = jnp.dot(acc.astype(bf16), wo_ref[...], preferred_element_type=f32)
    xn = _layer_norm(alpha * x_ref[...] + gm_ref[...] * out, lg_ref[...], lb_ref[...])
    xo_ref[...] = xn
    ho_ref[...] = (xn * (1.0 + scf_ref[...]) + shf_ref[...]).astype(ho_ref.dtype)


def _merge_call(cfg, ya, ya_ctx, yb, yc, gates, xs, nrows, wb, w_out, mods, layer, alpha, ln_g, ln_b):
    t = cfg.ts
    nlat_tiles = cfg.nlat // t
    row = lambda i: (i, 0)
    vec = pl.BlockSpec((1, cfg.D), lambda i: (0, 0))
    y_spec = pl.BlockSpec((t, cfg.BW), row)
    ya_specs = [pl.BlockSpec((t, cfg.BW), lambda i: (jnp.minimum(i, nlat_tiles - 1), 0))]
    ya_args = [ya]
    if ya_ctx is not None:
        ya_specs.append(pl.BlockSpec((t, cfg.BW), lambda i: (jnp.maximum(i - nlat_tiles, 0), 0)))
        ya_args.append(ya_ctx)
    return pl.pallas_call(
        functools.partial(_merge_kernel, alpha, nlat_tiles, ya_ctx is not None),
        grid=(nrows // t,),
        in_specs=[
            *ya_specs, y_spec, y_spec,
            pl.BlockSpec((t, 3 * cfg.D), row),
            pl.BlockSpec((None, 3, cfg.BW, cfg.D), lambda i: (layer, 0, 0, 0), pipeline_mode=pl.Buffered(1)),
            pl.BlockSpec((None, cfg.D, cfg.D), lambda i: (layer, 0, 0), pipeline_mode=pl.Buffered(1)),
            pl.BlockSpec((t, cfg.D), row),
            _mod_spec(cfg, layer, 2, t), _mod_spec(cfg, layer, 3, t), _mod_spec(cfg, layer, 4, t),
            vec, vec,
        ],
        out_specs=[pl.BlockSpec((t, cfg.D), row), pl.BlockSpec((t, cfg.D), row)],
        out_shape=[jax.ShapeDtypeStruct((nrows, cfg.D), f32), jax.ShapeDtypeStruct((nrows, cfg.D), bf16)],
        compiler_params=_params("parallel"),
        name="merge_outproj_ln",
    )(*ya_args, yb, yc, gates, wb, w_out, xs, mods, mods, mods, ln_g.reshape(1, cfg.D), ln_b.reshape(1, cfg.D))


def _ffn_up_kernel(h_ref, wg_ref, wu_ref, o_ref, wgb, wub):
    @pl.when(pl.program_id(1) == 0)
    def _():
        wgb[...] = wg_ref[...].astype(bf16)
        wub[...] = wu_ref[...].astype(bf16)

    h = h_ref[...]
    g = jnp.dot(h, wgb[...], preferred_element_type=f32)
    u = jnp.dot(h, wub[...], preferred_element_type=f32)
    o_ref[...] = (g * jax.nn.sigmoid(g) * u).astype(o_ref.dtype)


def _ffn_up_call(cfg, hf, nrows, w_gu, layer):
    tf = cfg.tf
    tm = max(t for t in (cfg.tm, nrows // 8) if nrows % t == 0 and t % (2 * SUBLANES) == 0)
    nj = cfg.FH // tf
    return pl.pallas_call(
        _ffn_up_kernel,
        grid=(nj, nrows // tm),
        in_specs=[
            pl.BlockSpec((tm, cfg.D), lambda j, i: (i, 0)),
            pl.BlockSpec((None, cfg.D, tf), lambda j, i: (layer, 0, j)),
            pl.BlockSpec((None, cfg.D, tf), lambda j, i: (layer, 0, nj + j)),
        ],
        out_specs=pl.BlockSpec((tm, tf), lambda j, i: (i, j)),
        out_shape=jax.ShapeDtypeStruct((nrows, cfg.FH), bf16),
        scratch_shapes=[pltpu.VMEM((cfg.D, tf), bf16), pltpu.VMEM((cfg.D, tf), bf16)],
        compiler_params=_params("parallel", "arbitrary"),
        name="ffn_up",
    )(hf, w_gu, w_gu)


def _ffn_down_kernel(alpha, has_next, act_ref, w_ref, x_ref, gf_ref, lg_ref, lb_ref, *rest):
    if has_next:
        shn_ref, scn_ref, xo_ref, ho_ref = rest
    else:
        (xo_ref,) = rest
    y = jnp.dot(act_ref[...], w_ref[...], preferred_element_type=f32)
    xn = _layer_norm(alpha * x_ref[...] + gf_ref[...] * y, lg_ref[...], lb_ref[...])
    xo_ref[...] = xn
    if has_next:
        ho_ref[...] = (xn * (1.0 + scn_ref[...]) + shn_ref[...]).astype(ho_ref.dtype)


def _ffn_down_call(cfg, act, xs, nrows, w_down, mods, layer, alpha, ln_g, ln_b):
    t = cfg.ts
    has_next = layer + 1 < cfg.DEPTH
    row = lambda i: (i, 0)
    vec = pl.BlockSpec((1, cfg.D), lambda i: (0, 0))
    in_specs = [
        pl.BlockSpec((t, cfg.FH), row),
        pl.BlockSpec((None, cfg.FH, cfg.D), lambda i: (layer, 0, 0), pipeline_mode=pl.Buffered(1)),
        pl.BlockSpec((t, cfg.D), row),
        _mod_spec(cfg, layer, 5, t), vec, vec,
    ]
    args = [act, w_down, xs, mods, ln_g.reshape(1, cfg.D), ln_b.reshape(1, cfg.D)]
    out_specs = [pl.BlockSpec((t, cfg.D), row)]
    out_shape = [jax.ShapeDtypeStruct((nrows, cfg.D), f32)]
    if has_next:
        in_specs += [_mod_spec(cfg, layer + 1, 0, t), _mod_spec(cfg, layer + 1, 1, t)]
        args += [mods, mods]
        out_specs.append(pl.BlockSpec((t, cfg.D), row))
        out_shape.append(jax.ShapeDtypeStruct((nrows, cfg.D), bf16))
    return pl.pallas_call(
        functools.partial(_ffn_down_kernel, alpha, has_next),
        grid=(nrows // t,),
        in_specs=in_specs,
        out_specs=out_specs,
        out_shape=out_shape,
        compiler_params=_params("parallel"),
        name="ffn_down_ln",
    )(*args)


def kernel(x, c, ctx, c_ctx, w_ada, b_ada, w_in, lam_qk, subln_g, gmlp_ln_g, gmlp_ln_b,
           w_spatial, b_spatial, w_pool, pool_scale, w_branch, w_out, ln1_g, ln1_b,
           w_gu, w_down, ln2_g, ln2_b):
    cfg = _make_cfg(x, ctx, w_ada, w_gu)
    D, BW = cfg.D, cfg.BW
    alpha = (2 * cfg.DEPTH) ** 0.25

    cond = jnp.zeros((MOD_ROWS, D), f32).at[:cfg.B].set(c).at[cfg.B].set(c_ctx)
    mods = _ada_call(cfg, cond, w_ada, b_ada).reshape(cfg.DEPTH * MOD_ROWS * N_MOD, 1, D)
    tables = _rope_tables(cfg)

    w_sp_b, w_pool_b = w_spatial.astype(bf16), w_pool.astype(bf16)
    w_br_b, w_out_b, w_down_b = w_branch.astype(bf16), w_out.astype(bf16), w_down.astype(bf16)

    xs, h = _entry_call(cfg, x.reshape(cfg.nlat, D), ctx.reshape(cfg.nctx, D), mods)

    for l in range(cfg.DEPTH):
        last = l == cfg.DEPTH - 1
        nrows = cfg.nlat if last else cfg.ntok
        lam_init = 0.8 - 0.6 * math.exp(-0.3 * l)

        qk = _inproj_call(cfg, h, w_in, l, 0, 2 * BW, cfg.ntok, "rope", bf16, tables)
        v = _inproj_call(cfg, h, w_in, l, 2 * BW, BW, cfg.ntok, "cast", bf16)
        zuv = _inproj_call(cfg, h, w_in, l, 3 * BW, 2 * BW, nrows, "gelu", f32)
        zc = _inproj_call(cfg, h, w_in, l, 5 * BW, BW, nrows, "copy", f32)
        gates = _inproj_call(cfg, h, w_in, l, 6 * BW, 3 * D, nrows, "sigmoid", f32)

        ya = _attn_call(cfg, qk, v, lam_qk[l], subln_g[l], lam_init, latent=True)
        ya_ctx = None if last else _attn_call(cfg, qk, v, lam_qk[l], subln_g[l], lam_init, latent=False)
        yb = _gmlp_call(cfg, zuv, nrows, gmlp_ln_g[l], gmlp_ln_b[l], w_sp_b[l], b_spatial[l].T)
        yc = _pool_call(cfg, zc, nrows, w_pool_b[l], pool_scale[l])

        xs, hf = _merge_call(cfg, ya, ya_ctx, yb, yc, gates, xs, nrows, w_br_b, w_out_b, mods, l, alpha,
                             ln1_g[l], ln1_b[l])
        act = _ffn_up_call(cfg, hf, nrows, w_gu, l)
        res = _ffn_down_call(cfg, act, xs, nrows, w_down_b, mods, l, alpha, ln2_g[l], ln2_b[l])
        if last:
            xs = res[0]
        else:
            xs, h = res

    return xs.reshape(cfg.B, cfg.S, D)
```

```python
import functools
import math
from typing import NamedTuple

import jax
import jax.numpy as jnp
from jax import lax
from jax.experimental import pallas as pl
from jax.experimental.pallas import tpu as pltpu

f32 = jnp.float32
bf16 = jnp.bfloat16

GRID_W = 64
A_DH = 64
A_DV = 2 * A_DH
ROPE_THETA = 10000.0
ROPE_PAIRS = A_DH // 4
A_SCALE = A_DH ** -0.5
B_CHUNK = 128
B_GROUPS = 8
C_WINDOWS = (2, 4, 8, 16)
C_HALO = 8
LN_EPS = 1e-6
N_MOD = 6
MOD_ROWS = 8
LOG2E = math.log2(math.e)

LANES = 128
SUBLANES = 8
VMEM_LIMIT_BYTES = 56 * 1024 * 1024


class Cfg(NamedTuple):
    B: int
    S: int
    CTX: int
    D: int
    DEPTH: int
    BW: int
    H: int
    FH: int
    nlat: int
    nctx: int
    ntok: int
    tm: int
    tr: int
    ts: int
    tq: int
    tql: int
    kchunk: int
    tf: int


def _pick(cands, *divs):
    for t in cands:
        if all(d % t == 0 for d in divs):
            return t
    raise ValueError(f"no tile in {cands} divides {divs}")


def _make_cfg(x, ctx, w_ada, w_gu):
    B, S, D = x.shape
    CTX = ctx.shape[1]
    BW = D // 2
    FH = w_gu.shape[2] // 2
    nlat, nctx = B * S, B * ctx.shape[1]
    assert B + 1 <= MOD_ROWS and S % GRID_W == 0 and nlat % CTX == 0
    assert CTX % B_CHUNK == 0 and S % B_CHUNK == 0 and BW % (4 * LANES) == 0
    tm = _pick((1024, 512, 256, 128), S, nctx)
    tr = _pick((512, 256, 128), S, nctx)
    ts = _pick((256, 128), S, nctx)
    tq = _pick((256, 128), S, CTX)
    tql = _pick((128,), S // 2)
    kchunk = _pick((1024, 512, 256, 128), S)
    tf = _pick((512, 256, 128), FH)
    return Cfg(B, S, CTX, D, w_ada.shape[0], BW, BW // A_DV, FH, nlat, nctx, nlat + nctx,
               tm, tr, ts, tq, tql, kchunk, tf)


def _params(*sem):
    return pltpu.CompilerParams(dimension_semantics=sem, vmem_limit_bytes=VMEM_LIMIT_BYTES)


def _mod_spec(cfg, layer, chunk, tile):
    nlat_tiles = cfg.nlat // tile

    def idx(i, *_):
        r = jnp.where(i < nlat_tiles, (i * tile) // cfg.S, cfg.B)
        return ((layer * MOD_ROWS + r) * N_MOD + chunk, 0, 0)

    return pl.BlockSpec((None, 1, cfg.D), idx)


def _layer_norm(xf, g, b):
    mu = jnp.mean(xf, -1, keepdims=True)
    xc = xf - mu
    var = jnp.mean(xc * xc, -1, keepdims=True)
    return xc * lax.rsqrt(var + LN_EPS) * g + b


def _ada_kernel(c_ref, w_ref, b_ref, o_ref):
    cv = c_ref[...]
    a = (cv * jax.nn.sigmoid(cv)).astype(bf16)
    o_ref[...] = jnp.dot(a, w_ref[...].astype(bf16), preferred_element_type=f32) + b_ref[...]


def _ada_call(cfg, cond, w_ada, b_ada):
    width = N_MOD * cfg.D
    tn = _pick((1024, 512, 256, 128), width)
    return pl.pallas_call(
        _ada_kernel,
        grid=(cfg.DEPTH, width // tn),
        in_specs=[
            pl.BlockSpec((MOD_ROWS, cfg.D), lambda l, j: (0, 0)),
            pl.BlockSpec((None, cfg.D, tn), lambda l, j: (l, 0, j)),
            pl.BlockSpec((None, 1, tn), lambda l, j: (l, 0, j)),
        ],
        out_specs=pl.BlockSpec((None, MOD_ROWS, tn), lambda l, j: (l, 0, j)),
        out_shape=jax.ShapeDtypeStruct((cfg.DEPTH, MOD_ROWS, width), f32),
        compiler_params=_params("parallel", "parallel"),
        name="adaln_table",
    )(cond, w_ada, b_ada.reshape(cfg.DEPTH, 1, width))


def _entry_kernel(nlat_tiles, x_ref, c_ref, sh_ref, sc_ref, xo_ref, ho_ref):
    def body(src_ref):
        xf = src_ref[...]
        mu = jnp.mean(xf, -1, keepdims=True)
        xc = xf - mu
        var = jnp.mean(xc * xc, -1, keepdims=True)
        xn = xc * lax.rsqrt(var + LN_EPS)
        xo_ref[...] = xn
        ho_ref[...] = (xn * (1.0 + sc_ref[...]) + sh_ref[...]).astype(bf16)

    i = pl.program_id(0)
    pl.when(i < nlat_tiles)(lambda: body(x_ref))
    pl.when(i >= nlat_tiles)(lambda: body(c_ref))


def _entry_call(cfg, x2, c2, mods):
    t = cfg.tr
    nlat_tiles = cfg.nlat // t
    row = lambda i: (i, 0)
    return pl.pallas_call(
        functools.partial(_entry_kernel, nlat_tiles),
        grid=(cfg.ntok // t,),
        in_specs=[
            pl.BlockSpec((t, cfg.D), lambda i: (jnp.minimum(i, nlat_tiles - 1), 0)),
            pl.BlockSpec((t, cfg.D), lambda i: (jnp.maximum(i - nlat_tiles, 0), 0)),
            _mod_spec(cfg, 0, 0, t),
            _mod_spec(cfg, 0, 1, t),
        ],
        out_specs=[pl.BlockSpec((t, cfg.D), row), pl.BlockSpec((t, cfg.D), row)],
        out_shape=[jax.ShapeDtypeStruct((cfg.ntok, cfg.D), f32),
                   jax.ShapeDtypeStruct((cfg.ntok, cfg.D), bf16)],
        compiler_params=_params("parallel"),
        name="entry_norm",
    )(x2, c2, mods, mods)


def _inproj_kernel(kind, h_ref, w_ref, *refs):
    o_ref, wb = refs[-2], refs[-1]

    @pl.when(pl.program_id(1) == 0)
    def _():
        wb[...] = w_ref[...].astype(bf16)

    acc = jnp.dot(h_ref[...], wb[...], preferred_element_type=f32)
    if kind == "rope":
        cos_ref, sin_ref = refs[0], refs[1]
        cs, sn = cos_ref[...], sin_ref[...]
        lane = lax.broadcasted_iota(jnp.int32, cs.shape, 1)
        first_half = (lane % (2 * ROPE_PAIRS)) < ROPE_PAIRS
        for cb in range(acc.shape[1] // LANES):
            cols = slice(cb * LANES, (cb + 1) * LANES)
            z = acc[:, cols]
            sw = jnp.where(first_half, pltpu.roll(z, LANES - ROPE_PAIRS, 1), pltpu.roll(z, ROPE_PAIRS, 1))
            o_ref[:, cols] = (z * cs + sw * sn).astype(o_ref.dtype)
    elif kind == "gelu":
        o_ref[...] = (0.5 * acc * (1.0 + lax.erf(acc * (2.0 ** -0.5)))).astype(o_ref.dtype)
    elif kind == "sigmoid":
        o_ref[...] = jax.nn.sigmoid(acc).astype(o_ref.dtype)
    else:
        o_ref[...] = acc.astype(o_ref.dtype)


def _inproj_call(cfg, h, w, layer, col0, ncols, nrows, kind, out_dtype, tables=None):
    tm = cfg.tm
    tn = _pick((1024, 512, 256, 128), ncols, col0)
    j0 = col0 // tn
    in_specs = [
        pl.BlockSpec((tm, cfg.D), lambda j, i: (i, 0)),
        pl.BlockSpec((None, cfg.D, tn), lambda j, i: (layer, 0, j0 + j)),
    ]
    args = [h, w]
    if kind == "rope":
        nlat_tiles, per_seq = cfg.nlat // tm, cfg.S // tm

        def tab_idx(j, i):
            return (j * tn // cfg.BW, jnp.where(i < nlat_tiles, i % per_seq, per_seq), 0)

        in_specs += [pl.BlockSpec((None, tm, LANES), tab_idx)] * 2
        args += list(tables)
    return pl.pallas_call(
        functools.partial(_inproj_kernel, kind),
        grid=(ncols // tn, nrows // tm),
        in_specs=in_specs,
        out_specs=pl.BlockSpec((tm, tn), lambda j, i: (i, j)),
        out_shape=jax.ShapeDtypeStruct((nrows, ncols), out_dtype),
        scratch_shapes=[pltpu.VMEM((cfg.D, tn), bf16)],
        compiler_params=_params("parallel", "arbitrary"),
        name="inproj_" + kind,
    )(*args)


def _rope_tables(cfg):
    t = jnp.arange(cfg.S)
    row, col = (t // GRID_W).astype(f32), (t % GRID_W).astype(f32)
    inv = ROPE_THETA ** (-jnp.arange(ROPE_PAIRS, dtype=f32) / ROPE_PAIRS)
    lane = jnp.arange(LANES)
    j = lane % A_DH
    axis, half, pair = j // (2 * ROPE_PAIRS), (j % (2 * ROPE_PAIRS)) // ROPE_PAIRS, j % ROPE_PAIRS
    ang = jnp.where(axis[None, :] == 0, row[:, None], col[:, None]) * inv[pair][None, :]
    cos = jnp.cos(ang)
    sin = jnp.sin(ang) * jnp.where(half == 0, -1.0, 1.0)[None, :]
    cos = jnp.concatenate([cos, jnp.ones((cfg.tm, LANES), f32)], 0)
    sin = jnp.concatenate([sin, jnp.zeros((cfg.tm, LANES), f32)], 0)
    qs = A_SCALE * LOG2E
    return jnp.stack([cos * qs, cos]), jnp.stack([sin * qs, sin])


_NT = (((1,), (1,)), ((), ()))


def _attn_kernel(tq, nq, n_ctx, n_lat, kchunk, lam_init, lq_ref, g_ref, q_ref, kc_ref, vc_ref, *rest):
    if n_lat:
        kl_ref, vl_ref, o_ref, s_a, s_b, m_a, m_b, vaug = rest
    else:
        o_ref, s_a, m_a, vaug = rest
    nk = n_ctx + n_lat

    vaug[0:n_ctx, 0:A_DV] = vc_ref[...]
    if n_lat:
        vaug[n_ctx:nk, 0:A_DV] = vl_ref[...]
    vaug[:, A_DV:2 * A_DV] = jnp.ones((nk, A_DV), bf16)

    chunks = [(kc_ref, 0, n_ctx, 0)]
    chunks += [(kl_ref, c * kchunk, kchunk, n_ctx + c * kchunk) for c in range(n_lat // kchunk)]

    lq = lq_ref[...]
    lam = (jnp.exp(jnp.sum(lq[0:1] * lq[1:2], axis=1, keepdims=True))
           - jnp.exp(jnp.sum(lq[2:3] * lq[3:4], axis=1, keepdims=True)) + lam_init)
    gain = g_ref[...] * (1.0 - lam_init)

    def tile_rows(j):
        return pl.ds(j * tq if isinstance(j, int) else pl.multiple_of(j * tq, tq), tq)

    def scores(j, s_buf, m_buf):
        q = q_ref[tile_rows(j), :]
        lane = lax.broadcasted_iota(jnp.int32, q.shape, 1)
        zero = jnp.zeros_like(q)
        qq = jnp.concatenate([jnp.where(lane < A_DH, q, zero), jnp.where(lane >= A_DH, q, zero)], axis=0)
        mx = None
        for ref, r0, n, c0 in chunks:
            s = lax.dot_general(qq, ref[r0:r0 + n, :], _NT, preferred_element_type=f32)
            s_buf[:, c0:c0 + n] = s
            for cb in range(n // LANES):
                blk = s[:, cb * LANES:(cb + 1) * LANES]
                mx = blk if mx is None else jnp.maximum(mx, blk)
        m_buf[...] = jnp.broadcast_to(jnp.max(mx, axis=1, keepdims=True), m_buf.shape)

    def values(j, s_buf, m_buf):
        m = m_buf[...]
        pv = None
        for ref, r0, n, c0 in chunks:
            e = jnp.concatenate(
                [jnp.exp2(s_buf[:, c0 + cb * LANES:c0 + (cb + 1) * LANES] - m) for cb in range(n // LANES)],
                axis=1).astype(bf16)
            d = jnp.dot(e, vaug[c0:c0 + n, :], preferred_element_type=f32)
            pv = d if pv is None else pv + d
        o = pv[:, 0:A_DV] / pv[:, A_DV:A_DV + 1]
        out = o[0:tq] - lam * o[tq:2 * tq]
        ms = jnp.mean(out * out, -1, keepdims=True)
        y = out * lax.rsqrt(ms + LN_EPS) * gain
        o_ref[tile_rows(j), :] = y.astype(o_ref.dtype)

    if nq == 1:
        scores(0, s_a, m_a)
        values(0, s_a, m_a)
        return

    scores(0, s_a, m_a)

    pairs = 2 if nq % 4 == 0 else 1

    def body(jj, carry):
        for p in range(pairs):
            j0 = 2 * (pairs * jj + p)
            scores(j0 + 1, s_b, m_b)
            values(j0, s_a, m_a)
            scores(jnp.minimum(j0 + 2, nq - 1), s_a, m_a)
            values(j0 + 1, s_b, m_b)
        return carry

    lax.fori_loop(0, nq // (2 * pairs), body, 0)


def _attn_t_kernel(tq, nq, n_ctx, n_lat, kchunk, lam_init, lq_ref, gt_ref, q_ref, kc_ref, vc_ref,
                   kl_ref, vl_ref, o_ref, s_a, s_b, m_a, m_b, vt):
    nk = n_ctx + n_lat
    vt[:, 0:n_ctx] = vc_ref[...].T
    for c in range(n_lat // kchunk):
        vt[:, n_ctx + c * kchunk:n_ctx + (c + 1) * kchunk] = vl_ref[c * kchunk:(c + 1) * kchunk, :].T

    chunks = [(kc_ref, 0, n_ctx, 0)]
    chunks += [(kl_ref, c * kchunk, kchunk, n_ctx + c * kchunk) for c in range(n_lat // kchunk)]

    lq = lq_ref[...]
    lam = (jnp.exp(jnp.sum(lq[0:1] * lq[1:2], axis=1, keepdims=True))
           - jnp.exp(jnp.sum(lq[2:3] * lq[3:4], axis=1, keepdims=True)) + lam_init)
    gain = gt_ref[...] * (1.0 - lam_init)

    def tile_rows(j):
        return pl.ds(pl.multiple_of(j * tq, tq), tq)

    def scores(j, s_buf, m_buf):
        q = q_ref[tile_rows(j), :]
        lane = lax.broadcasted_iota(jnp.int32, q.shape, 1)
        zero = jnp.zeros_like(q)
        qqt = jnp.concatenate([jnp.where(lane < A_DH, q, zero), jnp.where(lane >= A_DH, q, zero)], axis=0).T
        mx = None
        for ref, r0, n, c0 in chunks:
            st = jnp.dot(ref[r0:r0 + n, :], qqt, preferred_element_type=f32)
            s_buf[c0:c0 + n, :] = st
            cm = jnp.max(st, axis=0, keepdims=True)
            mx = cm if mx is None else jnp.maximum(mx, cm)
        m_buf[...] = jnp.broadcast_to(mx, m_buf.shape)

    def values(j, s_buf, m_buf):
        m = m_buf[0:1, :]
        acc, l = None, None
        for ref, r0, n, c0 in chunks:
            e = jnp.exp2(s_buf[c0:c0 + n, :] - m)
            ls = jnp.sum(e, axis=0, keepdims=True)
            d = jnp.dot(vt[:, c0:c0 + n], e.astype(bf16), preferred_element_type=f32)
            acc, l = (d, ls) if acc is None else (acc + d, l + ls)
        o = acc / l
        out = o[:, 0:tq] - lam * o[:, tq:2 * tq]
        ms = jnp.mean(out * out, axis=0, keepdims=True)
        y = out * lax.rsqrt(ms + LN_EPS) * gain
        o_ref[tile_rows(j), :] = y.T.astype(o_ref.dtype)

    scores(0, s_a, m_a)
    pairs = max(p for p in (4, 2, 1) if nq % (2 * p) == 0)

    def body(jj, carry):
        for p in range(pairs):
            j0 = 2 * (pairs * jj + p)
            scores(j0 + 1, s_b, m_b)
            values(j0, s_a, m_a)
            scores(jnp.minimum(j0 + 2, nq - 1), s_a, m_a)
            values(j0 + 1, s_b, m_b)
        return carry

    lax.fori_loop(0, nq // (2 * pairs), body, 0)


def _attn_call(cfg, qk, v, lam_qk_l, subln_g_l, lam_init, latent):
    H = cfg.H
    ctx_blk0 = cfg.nlat // cfg.CTX
    ctx_k = pl.BlockSpec((cfg.CTX, A_DV), lambda b, h: (ctx_blk0 + b, H + h))
    ctx_v = pl.BlockSpec((cfg.CTX, A_DV), lambda b, h: (ctx_blk0 + b, h))
    if latent:
        tq, nq, n_lat, rows = cfg.tql, cfg.S // cfg.tql, cfg.S, cfg.S
        assert nq % 2 == 0
        q_spec = pl.BlockSpec((rows, A_DV), lambda b, h: (b, h))
    else:
        tq, nq, n_lat, rows = cfg.CTX, 1, 0, cfg.CTX
        q_spec = pl.BlockSpec((rows, A_DV), lambda b, h: (ctx_blk0 + b, h))
    nk = cfg.CTX + n_lat
    if latent:
        body = _attn_t_kernel
        gain_shape = (A_DV, 1)
        extra_specs = [
            pl.BlockSpec((cfg.S, A_DV), lambda b, h: (b, H + h)),
            pl.BlockSpec((cfg.S, A_DV), lambda b, h: (b, h)),
        ]
        extra_args = [qk, v]
        scratch = [pltpu.VMEM((nk, 2 * tq), f32)] * 2 + [pltpu.VMEM((SUBLANES, 2 * tq), f32)] * 2
        scratch.append(pltpu.VMEM((A_DV, nk), bf16))
    else:
        body = _attn_kernel
        gain_shape = (1, A_DV)
        extra_specs, extra_args = [], []
        scratch = [pltpu.VMEM((2 * tq, nk), f32), pltpu.VMEM((2 * tq, LANES), f32),
                   pltpu.VMEM((nk, 2 * A_DV), bf16)]
    in_specs = [
        pl.BlockSpec((4, A_DH), lambda b, h: (0, 0)),
        pl.BlockSpec(gain_shape, lambda b, h: (0, 0)),
        q_spec, ctx_k, ctx_v, *extra_specs,
    ]
    args = [lam_qk_l, subln_g_l.reshape(gain_shape), qk, qk, v, *extra_args]
    return pl.pallas_call(
        functools.partial(body, tq, nq, cfg.CTX, n_lat, cfg.kchunk, lam_init),
        grid=(cfg.B, H),
        in_specs=in_specs,
        out_specs=pl.BlockSpec((rows, A_DV), lambda b, h: (b, h)),
        out_shape=jax.ShapeDtypeStruct((cfg.B * rows, cfg.BW), bf16),
        scratch_shapes=scratch,
        compiler_params=_params("parallel", "parallel"),
        name="diff_attn_lat" if latent else "diff_attn_ctx",
    )(*args)


def _gmlp_kernel(u_ref, v_ref, g_ref, b_ref, ws_ref, bs_ref, o_ref):
    vn = _layer_norm(v_ref[...], g_ref[...], b_ref[...]).astype(bf16)
    nch = vn.shape[0] // B_CHUNK
    gdim = vn.shape[1] // B_GROUPS
    for g in range(B_GROUPS):
        cols = slice(g * gdim, (g + 1) * gdim)
        vg = jnp.concatenate([vn[n * B_CHUNK:(n + 1) * B_CHUNK, cols] for n in range(nch)], axis=1)
        s = jnp.dot(ws_ref[g], vg, preferred_element_type=f32) + bs_ref[:, g:g + 1]
        for n in range(nch):
            rows = slice(n * B_CHUNK, (n + 1) * B_CHUNK)
            o_ref[rows, cols] = (u_ref[rows, cols] * s[:, n * gdim:(n + 1) * gdim]).astype(o_ref.dtype)


def _gmlp_call(cfg, zuv, nrows, ln_g, ln_b, ws_l, bs_t):
    t = cfg.tr
    return pl.pallas_call(
        _gmlp_kernel,
        grid=(nrows // t,),
        in_specs=[
            pl.BlockSpec((t, cfg.BW), lambda i: (i, 0)),
            pl.BlockSpec((t, cfg.BW), lambda i: (i, 1)),
            pl.BlockSpec((1, cfg.BW), lambda i: (0, 0)),
            pl.BlockSpec((1, cfg.BW), lambda i: (0, 0)),
            pl.BlockSpec((B_GROUPS, B_CHUNK, B_CHUNK), lambda i: (0, 0, 0)),
            pl.BlockSpec((B_CHUNK, B_GROUPS), lambda i: (0, 0)),
        ],
        out_specs=pl.BlockSpec((t, cfg.BW), lambda i: (i, 0)),
        out_shape=jax.ShapeDtypeStruct((nrows, cfg.BW), bf16),
        compiler_params=_params("parallel"),
        name="gmlp",
    )(zuv, zuv, ln_g.reshape(1, cfg.BW), ln_b.reshape(1, cfg.BW), ws_l, bs_t)


def _pool_kernel(cfg, t, cur_ref, prev_ref, next_ref, wp_ref, sc_ref, o_ref, ext, lv_a, lv_b):
    r0 = pl.program_id(0) * t
    is_lat = r0 < cfg.nlat
    n = jnp.where(is_lat, cfg.S, cfg.CTX)
    pos0 = jnp.where(is_lat, r0 % cfg.S, r0 % cfg.CTX)
    halo = lax.broadcasted_iota(jnp.int32, (C_HALO, 1), 0)
    ext[0:C_HALO, :] = jnp.where(pos0 - C_HALO + halo >= 0, prev_ref[...], 0.0)
    ext[C_HALO:C_HALO + t, :] = cur_ref[...]
    ext[C_HALO + t:2 * C_HALO + t, :] = jnp.where(pos0 + t + halo < n, next_ref[...], 0.0)
    pos = pos0 + lax.broadcasted_iota(jnp.int32, (t, 1), 0)
    gdim = cfg.BW // len(C_WINDOWS)
    for g, w in enumerate(C_WINDOWS):
        cols = slice(g * gdim, (g + 1) * gdim)
        nrow = t + 2 * C_HALO - 1
        lv_a[0:nrow, :] = ext[0:nrow, cols] + ext[1:nrow + 1, cols]
        src, dst, k = lv_a, lv_b, 2
        while k < w:
            nrow -= k
            dst[0:nrow, :] = src[0:nrow, :] + src[k:nrow + k, :]
            src, dst, k = dst, src, 2 * k
        acc = src[C_HALO - w // 2:C_HALO - w // 2 + t, :]
        lo = jnp.maximum(pos - w // 2, 0)
        hi = jnp.minimum(pos - w // 2 + w, n)
        inv = 1.0 / (hi - lo).astype(f32)
        d = (acc * inv - cur_ref[:, cols]).astype(bf16)
        y = jnp.dot(d, wp_ref[g], preferred_element_type=f32) * sc_ref[:, cols]
        o_ref[:, cols] = y.astype(o_ref.dtype)


def _pool_call(cfg, zc, nrows, wp_l, scale_l):
    t = cfg.tq
    hb = t // C_HALO
    last_hb = zc.shape[0] // C_HALO - 1
    gdim = cfg.BW // len(C_WINDOWS)
    return pl.pallas_call(
        functools.partial(_pool_kernel, cfg, t),
        grid=(nrows // t,),
        in_specs=[
            pl.BlockSpec((t, cfg.BW), lambda i: (i, 0)),
            pl.BlockSpec((C_HALO, cfg.BW), lambda i: (jnp.maximum(i * hb - 1, 0), 0)),
            pl.BlockSpec((C_HALO, cfg.BW), lambda i: (jnp.minimum((i + 1) * hb, last_hb), 0)),
            pl.BlockSpec((len(C_WINDOWS), gdim, gdim), lambda i: (0, 0, 0)),
            pl.BlockSpec((1, cfg.BW), lambda i: (0, 0)),
        ],
        out_specs=pl.BlockSpec((t, cfg.BW), lambda i: (i, 0)),
        out_shape=jax.ShapeDtypeStruct((nrows, cfg.BW), bf16),
        scratch_shapes=[pltpu.VMEM((t + 2 * C_HALO, cfg.BW), f32),
                        pltpu.VMEM((t + 2 * C_HALO, gdim), f32), pltpu.VMEM((t + 2 * C_HALO, gdim), f32)],
        compiler_params=_params("parallel"),
        name="pool",
    )(zc, zc, zc, wp_l, scale_l.reshape(1, cfg.BW))


def _merge_kernel(alpha, nlat_tiles, has_ctx, ya_ref, *refs):
    ya = ya_ref[...]
    if has_ctx:
        yac_ref, refs = refs[0], refs[1:]
        tile = jnp.full(ya.shape, pl.program_id(0), jnp.int32)
        ya = jnp.where(tile < nlat_tiles, ya, yac_ref[...])
    yb_ref, yc_ref, g_ref, wb_ref, wo_ref, x_ref, gm_ref, shf_ref, scf_ref, lg_ref, lb_ref, xo_ref, ho_ref = refs
    d = x_ref.shape[1]
    acc = None
    for n, y in enumerate((ya, yb_ref[...], yc_ref[...])):
        p = g_ref[:, n * d:(n + 1) * d] * jnp.dot(y, wb_ref[n], preferred_element_type=f32)
        acc = p if acc is None else acc + p
    out = jnp.dot(acc.astype(bf16), wo_ref[...], preferred_element_type=f32)
    xn = _layer_norm(alpha * x_ref[...] + gm_ref[...] * out, lg_ref[...], lb_ref[...])
    xo_ref[...] = xn
    ho_ref[...] = (xn * (1.0 + scf_ref[...]) + shf_ref[...]).astype(ho_ref.dtype)


def _merge_call(cfg, ya, ya_ctx, yb, yc, gates, xs, nrows, wb, w_out, mods, layer, alpha, ln_g, ln_b):
    t = cfg.ts
    nlat_tiles = cfg.nlat // t
    row = lambda i: (i, 0)
    vec = pl.BlockSpec((1, cfg.D), lambda i: (0, 0))
    y_spec = pl.BlockSpec((t, cfg.BW), row)
    ya_specs = [pl.BlockSpec((t, cfg.BW), lambda i: (jnp.minimum(i, nlat_tiles - 1), 0))]
    ya_args = [ya]
    if ya_ctx is not None:
        ya_specs.append(pl.BlockSpec((t, cfg.BW), lambda i: (jnp.maximum(i - nlat_tiles, 0), 0)))
        ya_args.append(ya_ctx)
    return pl.pallas_call(
        functools.partial(_merge_kernel, alpha, nlat_tiles, ya_ctx is not None),
        grid=(nrows // t,),
        in_specs=[
            *ya_specs, y_spec, y_spec,
            pl.BlockSpec((t, 3 * cfg.D), row),
            pl.BlockSpec((None, 3, cfg.BW, cfg.D), lambda i: (layer, 0, 0, 0), pipeline_mode=pl.Buffered(1)),
            pl.BlockSpec((None, cfg.D, cfg.D), lambda i: (layer, 0, 0), pipeline_mode=pl.Buffered(1)),
            pl.BlockSpec((t, cfg.D), row),
            _mod_spec(cfg, layer, 2, t), _mod_spec(cfg, layer, 3, t), _mod_spec(cfg, layer, 4, t),
            vec, vec,
        ],
        out_specs=[pl.BlockSpec((t, cfg.D), row), pl.BlockSpec((t, cfg.D), row)],
        out_shape=[jax.ShapeDtypeStruct((nrows, cfg.D), f32), jax.ShapeDtypeStruct((nrows, cfg.D), bf16)],
        compiler_params=_params("parallel"),
        name="merge_outproj_ln",
    )(*ya_args, yb, yc, gates, wb, w_out, xs, mods, mods, mods, ln_g.reshape(1, cfg.D), ln_b.reshape(1, cfg.D))


def _ffn_up_kernel(h_ref, wg_ref, wu_ref, o_ref, wgb, wub):
    @pl.when(pl.program_id(1) == 0)
    def _():
        wgb[...] = wg_ref[...].astype(bf16)
        wub[...] = wu_ref[...].astype(bf16)

    h = h_ref[...]
    g = jnp.dot(h, wgb[...], preferred_element_type=f32)
    u = jnp.dot(h, wub[...], preferred_element_type=f32)
    o_ref[...] = (g * jax.nn.sigmoid(g) * u).astype(o_ref.dtype)


def _ffn_up_call(cfg, hf, nrows, w_gu, layer):
    tm, tf = cfg.tm, cfg.tf
    nj = cfg.FH // tf
    return pl.pallas_call(
        _ffn_up_kernel,
        grid=(nj, nrows // tm),
        in_specs=[
            pl.BlockSpec((tm, cfg.D), lambda j, i: (i, 0)),
            pl.BlockSpec((None, cfg.D, tf), lambda j, i: (layer, 0, j)),
            pl.BlockSpec((None, cfg.D, tf), lambda j, i: (layer, 0, nj + j)),
        ],
        out_specs=pl.BlockSpec((tm, tf), lambda j, i: (i, j)),
        out_shape=jax.ShapeDtypeStruct((nrows, cfg.FH), bf16),
        scratch_shapes=[pltpu.VMEM((cfg.D, tf), bf16), pltpu.VMEM((cfg.D, tf), bf16)],
        compiler_params=_params("parallel", "arbitrary"),
        name="ffn_up",
    )(hf, w_gu, w_gu)


def _ffn_down_kernel(alpha, has_next, act_ref, w_ref, x_ref, gf_ref, lg_ref, lb_ref, *rest):
    if has_next:
        shn_ref, scn_ref, xo_ref, ho_ref = rest
    else:
        (xo_ref,) = rest
    y = jnp.dot(act_ref[...], w_ref[...], preferred_element_type=f32)
    xn = _layer_norm(alpha * x_ref[...] + gf_ref[...] * y, lg_ref[...], lb_ref[...])
    xo_ref[...] = xn
    if has_next:
        ho_ref[...] = (xn * (1.0 + scn_ref[...]) + shn_ref[...]).astype(ho_ref.dtype)


def _ffn_down_call(cfg, act, xs, nrows, w_down, mods, layer, alpha, ln_g, ln_b):
    t = cfg.ts
    has_next = layer + 1 < cfg.DEPTH
    row = lambda i: (i, 0)
    vec = pl.BlockSpec((1, cfg.D), lambda i: (0, 0))
    in_specs = [
        pl.BlockSpec((t, cfg.FH), row),
        pl.BlockSpec((None, cfg.FH, cfg.D), lambda i: (layer, 0, 0), pipeline_mode=pl.Buffered(1)),
        pl.BlockSpec((t, cfg.D), row),
        _mod_spec(cfg, layer, 5, t), vec, vec,
    ]
    args = [act, w_down, xs, mods, ln_g.reshape(1, cfg.D), ln_b.reshape(1, cfg.D)]
    out_specs = [pl.BlockSpec((t, cfg.D), row)]
    out_shape = [jax.ShapeDtypeStruct((nrows, cfg.D), f32)]
    if has_next:
        in_specs += [_mod_spec(cfg, layer + 1, 0, t), _mod_spec(cfg, layer + 1, 1, t)]
        args += [mods, mods]
        out_specs.append(pl.BlockSpec((t, cfg.D), row))
        out_shape.append(jax.ShapeDtypeStruct((nrows, cfg.D), bf16))
    return pl.pallas_call(
        functools.partial(_ffn_down_kernel, alpha, has_next),
        grid=(nrows // t,),
        in_specs=in_specs,
        out_specs=out_specs,
        out_shape=out_shape,
        compiler_params=_params("parallel"),
        name="ffn_down_ln",
    )(*args)


def kernel(x, c, ctx, c_ctx, w_ada, b_ada, w_in, lam_qk, subln_g, gmlp_ln_g, gmlp_ln_b,
           w_spatial, b_spatial, w_pool, pool_scale, w_branch, w_out, ln1_g, ln1_b,
           w_gu, w_down, ln2_g, ln2_b):
    cfg = _make_cfg(x, ctx, w_ada, w_gu)
    D, BW = cfg.D, cfg.BW
    alpha = (2 * cfg.DEPTH) ** 0.25

    cond = jnp.zeros((MOD_ROWS, D), f32).at[:cfg.B].set(c).at[cfg.B].set(c_ctx)
    mods = _ada_call(cfg, cond, w_ada, b_ada).reshape(cfg.DEPTH * MOD_ROWS * N_MOD, 1, D)
    tables = _rope_tables(cfg)

    w_sp_b, w_pool_b = w_spatial.astype(bf16), w_pool.astype(bf16)
    w_br_b, w_out_b, w_down_b = w_branch.astype(bf16), w_out.astype(bf16), w_down.astype(bf16)

    xs, h = _entry_call(cfg, x.reshape(cfg.nlat, D), ctx.reshape(cfg.nctx, D), mods)

    for l in range(cfg.DEPTH):
        last = l == cfg.DEPTH - 1
        nrows = cfg.nlat if last else cfg.ntok
        lam_init = 0.8 - 0.6 * math.exp(-0.3 * l)

        qk = _inproj_call(cfg, h, w_in, l, 0, 2 * BW, cfg.ntok, "rope", bf16, tables)
        v = _inproj_call(cfg, h, w_in, l, 2 * BW, BW, cfg.ntok, "cast", bf16)
        zuv = _inproj_call(cfg, h, w_in, l, 3 * BW, 2 * BW, nrows, "gelu", f32)
        zc = _inproj_call(cfg, h, w_in, l, 5 * BW, BW, nrows, "copy", f32)
        gates = _inproj_call(cfg, h, w_in, l, 6 * BW, 3 * D, nrows, "sigmoid", f32)

        ya = _attn_call(cfg, qk, v, lam_qk[l], subln_g[l], lam_init, latent=True)
        ya_ctx = None if last else _attn_call(cfg, qk, v, lam_qk[l], subln_g[l], lam_init, latent=False)
        yb = _gmlp_call(cfg, zuv, nrows, gmlp_ln_g[l], gmlp_ln_b[l], w_sp_b[l], b_spatial[l].T)
        yc = _pool_call(cfg, zc, nrows, w_pool_b[l], pool_scale[l])

        xs, hf = _merge_call(cfg, ya, ya_ctx, yb, yc, gates, xs, nrows, w_br_b, w_out_b, mods, l, alpha,
                             ln1_g[l], ln1_b[l])
        act = _ffn_up_call(cfg, hf, nrows, w_gu, l)
        res = _ffn_down_call(cfg, act, xs, nrows, w_down_b, mods, l, alpha, ln2_g[l], ln2_b[l])
        if last:
            xs = res[0]
        else:
            xs, h = res

    return xs.reshape(cfg.B, cfg.S, D)
```

```python
import functools
import math
from typing import NamedTuple

import jax
import jax.numpy as jnp
from jax import lax
from jax.experimental import pallas as pl
from jax.experimental.pallas import tpu as pltpu

f32 = jnp.float32
bf16 = jnp.bfloat16

GRID_W = 64
A_DH = 64
A_DV = 2 * A_DH
ROPE_THETA = 10000.0
ROPE_PAIRS = A_DH // 4
A_SCALE = A_DH ** -0.5
B_CHUNK = 128
B_GROUPS = 8
C_WINDOWS = (2, 4, 8, 16)
C_HALO = 8
LN_EPS = 1e-6
N_MOD = 6
MOD_ROWS = 8
LOG2E = math.log2(math.e)

LANES = 128
SUBLANES = 8
VMEM_LIMIT_BYTES = 56 * 1024 * 1024


class Cfg(NamedTuple):
    B: int
    S: int
    CTX: int
    D: int
    DEPTH: int
    BW: int
    H: int
    FH: int
    nlat: int
    nctx: int
    ntok: int
    tm: int
    tr: int
    ts: int
    tq: int
    tql: int
    kchunk: int
    tf: int


def _pick(cands, *divs):
    for t in cands:
        if all(d % t == 0 for d in divs):
            return t
    raise ValueError(f"no tile in {cands} divides {divs}")


def _make_cfg(x, ctx, w_ada, w_gu):
    B, S, D = x.shape
    CTX = ctx.shape[1]
    BW = D // 2
    FH = w_gu.shape[2] // 2
    nlat, nctx = B * S, B * ctx.shape[1]
    assert B + 1 <= MOD_ROWS and S % GRID_W == 0 and nlat % CTX == 0
    assert CTX % B_CHUNK == 0 and S % B_CHUNK == 0 and BW % (4 * LANES) == 0
    tm = _pick((1024, 512, 256, 128), S, nctx)
    tr = _pick((512, 256, 128), S, nctx)
    ts = _pick((256, 128), S, nctx)
    tq = _pick((256, 128), S, CTX)
    tql = _pick((128,), S // 2)
    kchunk = _pick((1024, 512, 256, 128), S)
    tf = _pick((512, 256, 128), FH)
    return Cfg(B, S, CTX, D, w_ada.shape[0], BW, BW // A_DV, FH, nlat, nctx, nlat + nctx,
               tm, tr, ts, tq, tql, kchunk, tf)


def _params(*sem):
    return pltpu.CompilerParams(dimension_semantics=sem, vmem_limit_bytes=VMEM_LIMIT_BYTES)


def _mod_spec(cfg, layer, chunk, tile):
    nlat_tiles = cfg.nlat // tile

    def idx(i, *_):
        r = jnp.where(i < nlat_tiles, (i * tile) // cfg.S, cfg.B)
        return ((layer * MOD_ROWS + r) * N_MOD + chunk, 0, 0)

    return pl.BlockSpec((None, 1, cfg.D), idx)


def _layer_norm(xf, g, b):
    mu = jnp.mean(xf, -1, keepdims=True)
    xc = xf - mu
    var = jnp.mean(xc * xc, -1, keepdims=True)
    return xc * lax.rsqrt(var + LN_EPS) * g + b


def _ada_kernel(c_ref, w_ref, b_ref, o_ref):
    cv = c_ref[...]
    a = (cv * jax.nn.sigmoid(cv)).astype(bf16)
    o_ref[...] = jnp.dot(a, w_ref[...].astype(bf16), preferred_element_type=f32) + b_ref[...]


def _ada_call(cfg, cond, w_ada, b_ada):
    width = N_MOD * cfg.D
    tn = _pick((1024, 512, 256, 128), width)
    return pl.pallas_call(
        _ada_kernel,
        grid=(cfg.DEPTH, width // tn),
        in_specs=[
            pl.BlockSpec((MOD_ROWS, cfg.D), lambda l, j: (0, 0)),
            pl.BlockSpec((None, cfg.D, tn), lambda l, j: (l, 0, j)),
            pl.BlockSpec((None, 1, tn), lambda l, j: (l, 0, j)),
        ],
        out_specs=pl.BlockSpec((None, MOD_ROWS, tn), lambda l, j: (l, 0, j)),
        out_shape=jax.ShapeDtypeStruct((cfg.DEPTH, MOD_ROWS, width), f32),
        compiler_params=_params("parallel", "parallel"),
        name="adaln_table",
    )(cond, w_ada, b_ada.reshape(cfg.DEPTH, 1, width))


def _entry_kernel(nlat_tiles, x_ref, c_ref, sh_ref, sc_ref, xo_ref, ho_ref):
    def body(src_ref):
        xf = src_ref[...]
        mu = jnp.mean(xf, -1, keepdims=True)
        xc = xf - mu
        var = jnp.mean(xc * xc, -1, keepdims=True)
        xn = xc * lax.rsqrt(var + LN_EPS)
        xo_ref[...] = xn
        ho_ref[...] = (xn * (1.0 + sc_ref[...]) + sh_ref[...]).astype(bf16)

    i = pl.program_id(0)
    pl.when(i < nlat_tiles)(lambda: body(x_ref))
    pl.when(i >= nlat_tiles)(lambda: body(c_ref))


def _entry_call(cfg, x2, c2, mods):
    t = cfg.tr
    nlat_tiles = cfg.nlat // t
    row = lambda i: (i, 0)
    return pl.pallas_call(
        functools.partial(_entry_kernel, nlat_tiles),
        grid=(cfg.ntok // t,),
        in_specs=[
            pl.BlockSpec((t, cfg.D), lambda i: (jnp.minimum(i, nlat_tiles - 1), 0)),
            pl.BlockSpec((t, cfg.D), lambda i: (jnp.maximum(i - nlat_tiles, 0), 0)),
            _mod_spec(cfg, 0, 0, t),
            _mod_spec(cfg, 0, 1, t),
        ],
        out_specs=[pl.BlockSpec((t, cfg.D), row), pl.BlockSpec((t, cfg.D), row)],
        out_shape=[jax.ShapeDtypeStruct((cfg.ntok, cfg.D), f32),
                   jax.ShapeDtypeStruct((cfg.ntok, cfg.D), bf16)],
        compiler_params=_params("parallel"),
        name="entry_norm",
    )(x2, c2, mods, mods)


def _inproj_kernel(kind, h_ref, w_ref, *refs):
    o_ref, wb = refs[-2], refs[-1]

    @pl.when(pl.program_id(1) == 0)
    def _():
        wb[...] = w_ref[...].astype(bf16)

    acc = jnp.dot(h_ref[...], wb[...], preferred_element_type=f32)
    if kind == "rope":
        cos_ref, sin_ref = refs[0], refs[1]
        cs, sn = cos_ref[...], sin_ref[...]
        lane = lax.broadcasted_iota(jnp.int32, cs.shape, 1)
        first_half = (lane % (2 * ROPE_PAIRS)) < ROPE_PAIRS
        for cb in range(acc.shape[1] // LANES):
            cols = slice(cb * LANES, (cb + 1) * LANES)
            z = acc[:, cols]
            sw = jnp.where(first_half, pltpu.roll(z, LANES - ROPE_PAIRS, 1), pltpu.roll(z, ROPE_PAIRS, 1))
            o_ref[:, cols] = (z * cs + sw * sn).astype(o_ref.dtype)
    elif kind == "gelu":
        o_ref[...] = (0.5 * acc * (1.0 + lax.erf(acc * (2.0 ** -0.5)))).astype(o_ref.dtype)
    elif kind == "sigmoid":
        o_ref[...] = jax.nn.sigmoid(acc).astype(o_ref.dtype)
    elif kind == "transpose":
        o_ref[...] = acc.T.astype(o_ref.dtype)
    else:
        o_ref[...] = acc.astype(o_ref.dtype)


def _inproj_call(cfg, h, w, layer, col0, ncols, nrows, kind, out_dtype, tables=None):
    tm = cfg.tm
    tn = _pick((1024, 512, 256, 128), ncols, col0)
    j0 = col0 // tn
    in_specs = [
        pl.BlockSpec((tm, cfg.D), lambda j, i: (i, 0)),
        pl.BlockSpec((None, cfg.D, tn), lambda j, i: (layer, 0, j0 + j)),
    ]
    args = [h, w]
    if kind == "rope":
        nlat_tiles, per_seq = cfg.nlat // tm, cfg.S // tm

        def tab_idx(j, i):
            return (j * tn // cfg.BW, jnp.where(i < nlat_tiles, i % per_seq, per_seq), 0)

        in_specs += [pl.BlockSpec((None, tm, LANES), tab_idx)] * 2
        args += list(tables)
    if kind == "transpose":
        out_spec = pl.BlockSpec((tn, tm), lambda j, i: (j, i))
        out_shape = jax.ShapeDtypeStruct((ncols, nrows), out_dtype)
    else:
        out_spec = pl.BlockSpec((tm, tn), lambda j, i: (i, j))
        out_shape = jax.ShapeDtypeStruct((nrows, ncols), out_dtype)
    return pl.pallas_call(
        functools.partial(_inproj_kernel, kind),
        grid=(ncols // tn, nrows // tm),
        in_specs=in_specs,
        out_specs=out_spec,
        out_shape=out_shape,
        scratch_shapes=[pltpu.VMEM((cfg.D, tn), bf16)],
        compiler_params=_params("parallel", "arbitrary"),
        name="inproj_" + kind,
    )(*args)


def _rope_tables(cfg):
    t = jnp.arange(cfg.S)
    row, col = (t // GRID_W).astype(f32), (t % GRID_W).astype(f32)
    inv = ROPE_THETA ** (-jnp.arange(ROPE_PAIRS, dtype=f32) / ROPE_PAIRS)
    lane = jnp.arange(LANES)
    j = lane % A_DH
    axis, half, pair = j // (2 * ROPE_PAIRS), (j % (2 * ROPE_PAIRS)) // ROPE_PAIRS, j % ROPE_PAIRS
    ang = jnp.where(axis[None, :] == 0, row[:, None], col[:, None]) * inv[pair][None, :]
    cos = jnp.cos(ang)
    sin = jnp.sin(ang) * jnp.where(half == 0, -1.0, 1.0)[None, :]
    cos = jnp.concatenate([cos, jnp.ones((cfg.tm, LANES), f32)], 0)
    sin = jnp.concatenate([sin, jnp.zeros((cfg.tm, LANES), f32)], 0)
    qs = A_SCALE * LOG2E
    return jnp.stack([cos * qs, cos]), jnp.stack([sin * qs, sin])


_NT = (((1,), (1,)), ((), ()))


def _attn_ctx_kernel(tq, lam_init, lq_ref, g_ref, q_ref, kc_ref, vc_ref, o_ref):
    lq = lq_ref[...]
    lam = (jnp.exp(jnp.sum(lq[0:1] * lq[1:2], axis=1, keepdims=True))
           - jnp.exp(jnp.sum(lq[2:3] * lq[3:4], axis=1, keepdims=True)) + lam_init)
    gain = g_ref[...] * (1.0 - lam_init)

    q = q_ref[...]
    lane = lax.broadcasted_iota(jnp.int32, q.shape, 1)
    zero = jnp.zeros_like(q)
    qq = jnp.concatenate([jnp.where(lane < A_DH, q, zero), jnp.where(lane >= A_DH, q, zero)], axis=0)
    s = lax.dot_general(qq, kc_ref[...], _NT, preferred_element_type=f32)
    e = jnp.exp2(s - jnp.max(s, axis=1, keepdims=True)).astype(bf16)
    v = vc_ref[...].T
    vaug = jnp.concatenate([v, jnp.ones_like(v)], axis=1)
    pv = jnp.dot(e, vaug, preferred_element_type=f32)
    o = pv[:, 0:A_DV] / pv[:, A_DV:A_DV + 1]
    out = o[0:tq] - lam * o[tq:2 * tq]
    ms = jnp.mean(out * out, -1, keepdims=True)
    o_ref[...] = (out * lax.rsqrt(ms + LN_EPS) * gain).astype(o_ref.dtype)


def _attn_t_kernel(tq, nq, n_ctx, n_lat, kchunk, lam_init, lq_ref, gt_ref, q_ref, kc_ref, vc_ref,
                   kl_ref, vl_ref, o_ref, s_a, s_b, m_a, m_b):
    chunks = [(kc_ref, vc_ref, 0, n_ctx, 0)]
    chunks += [(kl_ref, vl_ref, c * kchunk, kchunk, n_ctx + c * kchunk) for c in range(n_lat // kchunk)]

    lq = lq_ref[...]
    lam = (jnp.exp(jnp.sum(lq[0:1] * lq[1:2], axis=1, keepdims=True))
           - jnp.exp(jnp.sum(lq[2:3] * lq[3:4], axis=1, keepdims=True)) + lam_init)
    gain = gt_ref[...] * (1.0 - lam_init)

    def tile_rows(j):
        return pl.ds(j * tq if isinstance(j, int) else pl.multiple_of(j * tq, tq), tq)

    def scores(j, s_buf, m_buf):
        q = q_ref[tile_rows(j), :]
        lane = lax.broadcasted_iota(jnp.int32, q.shape, 1)
        zero = jnp.zeros_like(q)
        qqt = jnp.concatenate([jnp.where(lane < A_DH, q, zero), jnp.where(lane >= A_DH, q, zero)], axis=0).T
        mx = None
        for k_ref, _, r0, n, c0 in chunks:
            st = jnp.dot(k_ref[r0:r0 + n, :], qqt, preferred_element_type=f32)
            s_buf[c0:c0 + n, :] = st
            cm = jnp.max(st, axis=0, keepdims=True)
            mx = cm if mx is None else jnp.maximum(mx, cm)
        m_buf[...] = jnp.broadcast_to(mx, m_buf.shape)

    def values(j, s_buf, m_buf):
        m = m_buf[0:1, :]
        acc, l = None, None
        for _, vt_ref, r0, n, c0 in chunks:
            e = jnp.exp2(s_buf[c0:c0 + n, :] - m)
            ls = jnp.sum(e, axis=0, keepdims=True)
            d = jnp.dot(vt_ref[:, r0:r0 + n], e.astype(bf16), preferred_element_type=f32)
            acc, l = (d, ls) if acc is None else (acc + d, l + ls)
        o = acc / l
        out = o[:, 0:tq] - lam * o[:, tq:2 * tq]
        ms = jnp.mean(out * out, axis=0, keepdims=True)
        y = out * lax.rsqrt(ms + LN_EPS) * gain
        o_ref[tile_rows(j), :] = y.T.astype(o_ref.dtype)

    scores(0, s_a, m_a)
    pairs = max(p for p in (4, 2, 1) if nq % (2 * p) == 0)
    trips = nq // (2 * pairs)

    def trip(jj, final):
        for p in range(pairs):
            j0 = 2 * (pairs * jj + p)
            scores(j0 + 1, s_b, m_b)
            values(j0, s_a, m_a)
            if not (final and p == pairs - 1):
                scores(j0 + 2, s_a, m_a)
            values(j0 + 1, s_b, m_b)

    def body(jj, carry):
        trip(jj, False)
        return carry

    lax.fori_loop(0, trips - 1, body, 0)
    trip(trips - 1, True)


def _attn_call(cfg, qk, vt, lam_qk_l, subln_g_l, lam_init, latent):
    H = cfg.H
    ctx_blk0 = cfg.nlat // cfg.CTX
    ctx_k = pl.BlockSpec((cfg.CTX, A_DV), lambda b, h: (ctx_blk0 + b, H + h))
    ctx_v = pl.BlockSpec((A_DV, cfg.CTX), lambda b, h: (h, ctx_blk0 + b))
    if latent:
        tq, nq, n_lat, rows = cfg.tql, cfg.S // cfg.tql, cfg.S, cfg.S
        assert nq % 2 == 0
        q_spec = pl.BlockSpec((rows, A_DV), lambda b, h: (b, h))
    else:
        tq, nq, n_lat, rows = cfg.CTX, 1, 0, cfg.CTX
        q_spec = pl.BlockSpec((rows, A_DV), lambda b, h: (ctx_blk0 + b, h))
    nk = cfg.CTX + n_lat
    if latent:
        body = functools.partial(_attn_t_kernel, tq, nq, cfg.CTX, n_lat, cfg.kchunk, lam_init)
        gain_shape = (A_DV, 1)
        extra_specs = [
            pl.BlockSpec((cfg.S, A_DV), lambda b, h: (b, H + h)),
            pl.BlockSpec((A_DV, cfg.S), lambda b, h: (h, b)),
        ]
        extra_args = [qk, vt]
        scratch = [pltpu.VMEM((nk, 2 * tq), f32)] * 2 + [pltpu.VMEM((SUBLANES, 2 * tq), f32)] * 2
    else:
        body = functools.partial(_attn_ctx_kernel, tq, lam_init)
        gain_shape = (1, A_DV)
        extra_specs, extra_args, scratch = [], [], []
    in_specs = [
        pl.BlockSpec((4, A_DH), lambda b, h: (0, 0)),
        pl.BlockSpec(gain_shape, lambda b, h: (0, 0)),
        q_spec, ctx_k, ctx_v, *extra_specs,
    ]
    args = [lam_qk_l, subln_g_l.reshape(gain_shape), qk, qk, vt, *extra_args]
    return pl.pallas_call(
        body,
        grid=(cfg.B, H),
        in_specs=in_specs,
        out_specs=pl.BlockSpec((rows, A_DV), lambda b, h: (b, h)),
        out_shape=jax.ShapeDtypeStruct((cfg.B * rows, cfg.BW), bf16),
        scratch_shapes=scratch,
        compiler_params=_params("parallel", "parallel"),
        name="diff_attn_lat" if latent else "diff_attn_ctx",
    )(*args)


def _gmlp_kernel(u_ref, v_ref, g_ref, b_ref, ws_ref, bs_ref, o_ref):
    vn = _layer_norm(v_ref[...], g_ref[...], b_ref[...]).astype(bf16)
    nch = vn.shape[0] // B_CHUNK
    gdim = vn.shape[1] // B_GROUPS
    for g in range(B_GROUPS):
        cols = slice(g * gdim, (g + 1) * gdim)
        vg = jnp.concatenate([vn[n * B_CHUNK:(n + 1) * B_CHUNK, cols] for n in range(nch)], axis=1)
        s = jnp.dot(ws_ref[g], vg, preferred_element_type=f32) + bs_ref[:, g:g + 1]
        for n in range(nch):
            rows = slice(n * B_CHUNK, (n + 1) * B_CHUNK)
            o_ref[rows, cols] = (u_ref[rows, cols] * s[:, n * gdim:(n + 1) * gdim]).astype(o_ref.dtype)


def _gmlp_call(cfg, zuv, nrows, ln_g, ln_b, ws_l, bs_t):
    t = cfg.tr
    return pl.pallas_call(
        _gmlp_kernel,
        grid=(nrows // t,),
        in_specs=[
            pl.BlockSpec((t, cfg.BW), lambda i: (i, 0)),
            pl.BlockSpec((t, cfg.BW), lambda i: (i, 1)),
            pl.BlockSpec((1, cfg.BW), lambda i: (0, 0)),
            pl.BlockSpec((1, cfg.BW), lambda i: (0, 0)),
            pl.BlockSpec((B_GROUPS, B_CHUNK, B_CHUNK), lambda i: (0, 0, 0)),
            pl.BlockSpec((B_CHUNK, B_GROUPS), lambda i: (0, 0)),
        ],
        out_specs=pl.BlockSpec((t, cfg.BW), lambda i: (i, 0)),
        out_shape=jax.ShapeDtypeStruct((nrows, cfg.BW), bf16),
        compiler_params=_params("parallel"),
        name="gmlp",
    )(zuv, zuv, ln_g.reshape(1, cfg.BW), ln_b.reshape(1, cfg.BW), ws_l, bs_t)


def _pool_kernel(cfg, t, cur_ref, prev_ref, next_ref, wp_ref, sc_ref, o_ref, ext, lv_a, lv_b):
    r0 = pl.program_id(0) * t
    is_lat = r0 < cfg.nlat
    n = jnp.where(is_lat, cfg.S, cfg.CTX)
    pos0 = jnp.where(is_lat, r0 % cfg.S, r0 % cfg.CTX)
    halo = lax.broadcasted_iota(jnp.int32, (C_HALO, 1), 0)
    ext[0:C_HALO, :] = jnp.where(pos0 - C_HALO + halo >= 0, prev_ref[...], 0.0)
    ext[C_HALO:C_HALO + t, :] = cur_ref[...]
    ext[C_HALO + t:2 * C_HALO + t, :] = jnp.where(pos0 + t + halo < n, next_ref[...], 0.0)
    pos = pos0 + lax.broadcasted_iota(jnp.int32, (t, 1), 0)
    gdim = cfg.BW // len(C_WINDOWS)
    for g, w in enumerate(C_WINDOWS):
        cols = slice(g * gdim, (g + 1) * gdim)
        nrow = t + 2 * C_HALO - 1
        lv_a[0:nrow, :] = ext[0:nrow, cols] + ext[1:nrow + 1, cols]
        src, dst, k = lv_a, lv_b, 2
        while k < w:
            nrow -= k
            dst[0:nrow, :] = src[0:nrow, :] + src[k:nrow + k, :]
            src, dst, k = dst, src, 2 * k
        acc = src[C_HALO - w // 2:C_HALO - w // 2 + t, :]
        lo = jnp.maximum(pos - w // 2, 0)
        hi = jnp.minimum(pos - w // 2 + w, n)
        inv = 1.0 / (hi - lo).astype(f32)
        d = (acc * inv - cur_ref[:, cols]).astype(bf16)
        y = jnp.dot(d, wp_ref[g], preferred_element_type=f32) * sc_ref[:, cols]
        o_ref[:, cols] = y.astype(o_ref.dtype)


def _pool_call(cfg, zc, nrows, wp_l, scale_l):
    t = cfg.tq
    hb = t // C_HALO
    last_hb = zc.shape[0] // C_HALO - 1
    gdim = cfg.BW // len(C_WINDOWS)
    return pl.pallas_call(
        functools.partial(_pool_kernel, cfg, t),
        grid=(nrows // t,),
        in_specs=[
            pl.BlockSpec((t, cfg.BW), lambda i: (i, 0)),
            pl.BlockSpec((C_HALO, cfg.BW), lambda i: (jnp.maximum(i * hb - 1, 0), 0)),
            pl.BlockSpec((C_HALO, cfg.BW), lambda i: (jnp.minimum((i + 1) * hb, last_hb), 0)),
            pl.BlockSpec((len(C_WINDOWS), gdim, gdim), lambda i: (0, 0, 0)),
            pl.BlockSpec((1, cfg.BW), lambda i: (0, 0)),
        ],
        out_specs=pl.BlockSpec((t, cfg.BW), lambda i: (i, 0)),
        out_shape=jax.ShapeDtypeStruct((nrows, cfg.BW), bf16),
        scratch_shapes=[pltpu.VMEM((t + 2 * C_HALO, cfg.BW), f32),
                        pltpu.VMEM((t + 2 * C_HALO, gdim), f32), pltpu.VMEM((t + 2 * C_HALO, gdim), f32)],
        compiler_params=_params("parallel"),
        name="pool",
    )(zc, zc, zc, wp_l, scale_l.reshape(1, cfg.BW))


def _merge_kernel(alpha, nlat_tiles, has_ctx, ya_ref, *refs):
    ya = ya_ref[...]
    if has_ctx:
        yac_ref, refs = refs[0], refs[1:]
        tile = jnp.full(ya.shape, pl.program_id(0), jnp.int32)
        ya = jnp.where(tile < nlat_tiles, ya, yac_ref[...])
    yb_ref, yc_ref, g_ref, wb_ref, wo_ref, x_ref, gm_ref, shf_ref, scf_ref, lg_ref, lb_ref, xo_ref, ho_ref = refs
    d = x_ref.shape[1]
    acc = None
    for n, y in enumerate((ya, yb_ref[...], yc_ref[...])):
        p = g_ref[:, n * d:(n + 1) * d] * jnp.dot(y, wb_ref[n], preferred_element_type=f32)
        acc = p if acc is None else acc + p
    out = jnp.dot(acc.astype(bf16), wo_ref[...], preferred_element_type=f32)
    xn = _layer_norm(alpha * x_ref[...] + gm_ref[...] * out, lg_ref[...], lb_ref[...])
    xo_ref[...] = xn
    ho_ref[...] = (xn * (1.0 + scf_ref[...]) + shf_ref[...]).astype(ho_ref.dtype)


def _merge_call(cfg, ya, ya_ctx, yb, yc, gates, xs, nrows, wb, w_out, mods, layer, alpha, ln_g, ln_b):
    t = cfg.ts
    nlat_tiles = cfg.nlat // t
    row = lambda i: (i, 0)
    vec = pl.BlockSpec((1, cfg.D), lambda i: (0, 0))
    y_spec = pl.BlockSpec((t, cfg.BW), row)
    ya_specs = [pl.BlockSpec((t, cfg.BW), lambda i: (jnp.minimum(i, nlat_tiles - 1), 0))]
    ya_args = [ya]
    if ya_ctx is not None:
        ya_specs.append(pl.BlockSpec((t, cfg.BW), lambda i: (jnp.maximum(i - nlat_tiles, 0), 0)))
        ya_args.append(ya_ctx)
    return pl.pallas_call(
        functools.partial(_merge_kernel, alpha, nlat_tiles, ya_ctx is not None),
        grid=(nrows // t,),
        in_specs=[
            *ya_specs, y_spec, y_spec,
            pl.BlockSpec((t, 3 * cfg.D), row),
            pl.BlockSpec((None, 3, cfg.BW, cfg.D), lambda i: (layer, 0, 0, 0), pipeline_mode=pl.Buffered(1)),
            pl.BlockSpec((None, cfg.D, cfg.D), lambda i: (layer, 0, 0), pipeline_mode=pl.Buffered(1)),
            pl.BlockSpec((t, cfg.D), row),
            _mod_spec(cfg, layer, 2, t), _mod_spec(cfg, layer, 3, t), _mod_spec(cfg, layer, 4, t),
            vec, vec,
        ],
        out_specs=[pl.BlockSpec((t, cfg.D), row), pl.BlockSpec((t, cfg.D), row)],
        out_shape=[jax.ShapeDtypeStruct((nrows, cfg.D), f32), jax.ShapeDtypeStruct((nrows, cfg.D), bf16)],
        compiler_params=_params("parallel"),
        name="merge_outproj_ln",
    )(*ya_args, yb, yc, gates, wb, w_out, xs, mods, mods, mods, ln_g.reshape(1, cfg.D), ln_b.reshape(1, cfg.D))


def _ffn_up_kernel(h_ref, wg_ref, wu_ref, o_ref, wgb, wub):
    @pl.when(pl.program_id(1) == 0)
    def _():
        wgb[...] = wg_ref[...].astype(bf16)
        wub[...] = wu_ref[...].astype(bf16)

    h = h_ref[...]
    g = jnp.dot(h, wgb[...], preferred_element_type=f32)
    u = jnp.dot(h, wub[...], preferred_element_type=f32)
    o_ref[...] = (g * jax.nn.sigmoid(g) * u).astype(o_ref.dtype)


def _ffn_up_call(cfg, hf, nrows, w_gu, layer):
    tm, tf = cfg.tm, cfg.tf
    nj = cfg.FH // tf
    return pl.pallas_call(
        _ffn_up_kernel,
        grid=(nj, nrows // tm),
        in_specs=[
            pl.BlockSpec((tm, cfg.D), lambda j, i: (i, 0)),
            pl.BlockSpec((None, cfg.D, tf), lambda j, i: (layer, 0, j)),
            pl.BlockSpec((None, cfg.D, tf), lambda j, i: (layer, 0, nj + j)),
        ],
        out_specs=pl.BlockSpec((tm, tf), lambda j, i: (i, j)),
        out_shape=jax.ShapeDtypeStruct((nrows, cfg.FH), bf16),
        scratch_shapes=[pltpu.VMEM((cfg.D, tf), bf16), pltpu.VMEM((cfg.D, tf), bf16)],
        compiler_params=_params("parallel", "arbitrary"),
        name="ffn_up",
    )(hf, w_gu, w_gu)


def _ffn_down_kernel(alpha, has_next, act_ref, w_ref, x_ref, gf_ref, lg_ref, lb_ref, *rest):
    if has_next:
        shn_ref, scn_ref, xo_ref, ho_ref = rest
    else:
        (xo_ref,) = rest
    y = jnp.dot(act_ref[...], w_ref[...], preferred_element_type=f32)
    xn = _layer_norm(alpha * x_ref[...] + gf_ref[...] * y, lg_ref[...], lb_ref[...])
    xo_ref[...] = xn
    if has_next:
        ho_ref[...] = (xn * (1.0 + scn_ref[...]) + shn_ref[...]).astype(ho_ref.dtype)


def _ffn_down_call(cfg, act, xs, nrows, w_down, mods, layer, alpha, ln_g, ln_b):
    t = cfg.ts
    has_next = layer + 1 < cfg.DEPTH
    row = lambda i: (i, 0)
    vec = pl.BlockSpec((1, cfg.D), lambda i: (0, 0))
    in_specs = [
        pl.BlockSpec((t, cfg.FH), row),
        pl.BlockSpec((None, cfg.FH, cfg.D), lambda i: (layer, 0, 0), pipeline_mode=pl.Buffered(1)),
        pl.BlockSpec((t, cfg.D), row),
        _mod_spec(cfg, layer, 5, t), vec, vec,
    ]
    args = [act, w_down, xs, mods, ln_g.reshape(1, cfg.D), ln_b.reshape(1, cfg.D)]
    out_specs = [pl.BlockSpec((t, cfg.D), row)]
    out_shape = [jax.ShapeDtypeStruct((nrows, cfg.D), f32)]
    if has_next:
        in_specs += [_mod_spec(cfg, layer + 1, 0, t), _mod_spec(cfg, layer + 1, 1, t)]
        args += [mods, mods]
        out_specs.append(pl.BlockSpec((t, cfg.D), row))
        out_shape.append(jax.ShapeDtypeStruct((nrows, cfg.D), bf16))
    return pl.pallas_call(
        functools.partial(_ffn_down_kernel, alpha, has_next),
        grid=(nrows // t,),
        in_specs=in_specs,
        out_specs=out_specs,
        out_shape=out_shape,
        compiler_params=_params("parallel"),
        name="ffn_down_ln",
    )(*args)


def kernel(x, c, ctx, c_ctx, w_ada, b_ada, w_in, lam_qk, subln_g, gmlp_ln_g, gmlp_ln_b,
           w_spatial, b_spatial, w_pool, pool_scale, w_branch, w_out, ln1_g, ln1_b,
           w_gu, w_down, ln2_g, ln2_b):
    cfg = _make_cfg(x, ctx, w_ada, w_gu)
    D, BW = cfg.D, cfg.BW
    alpha = (2 * cfg.DEPTH) ** 0.25

    cond = jnp.zeros((MOD_ROWS, D), f32).at[:cfg.B].set(c).at[cfg.B].set(c_ctx)
    mods = _ada_call(cfg, cond, w_ada, b_ada).reshape(cfg.DEPTH * MOD_ROWS * N_MOD, 1, D)
    tables = _rope_tables(cfg)

    w_sp_b, w_pool_b = w_spatial.astype(bf16), w_pool.astype(bf16)
    w_br_b, w_out_b, w_down_b = w_branch.astype(bf16), w_out.astype(bf16), w_down.astype(bf16)

    xs, h = _entry_call(cfg, x.reshape(cfg.nlat, D), ctx.reshape(cfg.nctx, D), mods)

    for l in range(cfg.DEPTH):
        last = l == cfg.DEPTH - 1
        nrows = cfg.nlat if last else cfg.ntok
        lam_init = 0.8 - 0.6 * math.exp(-0.3 * l)

        qk = _inproj_call(cfg, h, w_in, l, 0, 2 * BW, cfg.ntok, "rope", bf16, tables)
        vt = _inproj_call(cfg, h, w_in, l, 2 * BW, BW, cfg.ntok, "transpose", bf16)
        zuv = _inproj_call(cfg, h, w_in, l, 3 * BW, 2 * BW, nrows, "gelu", f32)
        zc = _inproj_call(cfg, h, w_in, l, 5 * BW, BW, nrows, "copy", f32)
        gates = _inproj_call(cfg, h, w_in, l, 6 * BW, 3 * D, nrows, "sigmoid", f32)

        ya = _attn_call(cfg, qk, vt, lam_qk[l], subln_g[l], lam_init, latent=True)
        ya_ctx = None if last else _attn_call(cfg, qk, vt, lam_qk[l], subln_g[l], lam_init, latent=False)
        yb = _gmlp_call(cfg, zuv, nrows, gmlp_ln_g[l], gmlp_ln_b[l], w_sp_b[l], b_spatial[l].T)
        yc = _pool_call(cfg, zc, nrows, w_pool_b[l], pool_scale[l])

        xs, hf = _merge_call(cfg, ya, ya_ctx, yb, yc, gates, xs, nrows, w_br_b, w_out_b, mods, l, alpha,
                             ln1_g[l], ln1_b[l])
        act = _ffn_up_call(cfg, hf, nrows, w_gu, l)
        res = _ffn_down_call(cfg, act, xs, nrows, w_down_b, mods, l, alpha, ln2_g[l], ln2_b[l])
        if last:
            xs = res[0]
        else:
            xs, h = res

    return xs.reshape(cfg.B, cfg.S, D)
```
